```python
import jax, jax.numpy as jnp
from jax import lax
import numpy as np

D_MODEL = 2048
BATCH = 4
SEQ = 8192
DEPTH = 4

N_A_LAYERS = DEPTH // 2
N_B_LAYERS = DEPTH - N_A_LAYERS
GM_WIDTH = 2 * D_MODEL
GM_GROUPS = 16
GM_GROUP_DIM = GM_WIDTH // GM_GROUPS
CHUNK = 128
N_HEADS = 16
QK_NOPE_DIM = 128
QK_ROPE_DIM = 64
V_HEAD_DIM = 128
Q_LORA_RANK = 512
KV_LORA_RANK = 512
ATTN_WIDTH = N_HEADS * V_HEAD_DIM
ROPE_THETA = 10000.0
Q_BLOCK = 128
EPS = 1e-6

kernel_name = "yoco_gmlp_mla_adaln_trunk"


def rms_norm(x, g):
    xf = x.astype(jnp.float32)
    y = xf * lax.rsqrt(jnp.mean(xf * xf, axis=-1, keepdims=True) + EPS)
    return (y * g.astype(jnp.float32)).astype(x.dtype)


def layer_norm(x, g, b):
    xf = x.astype(jnp.float32)
    mu = jnp.mean(xf, axis=-1, keepdims=True)
    xc = xf - mu
    y = xc * lax.rsqrt(jnp.mean(xc * xc, axis=-1, keepdims=True) + EPS)
    return (y * g.astype(jnp.float32) + b.astype(jnp.float32)).astype(x.dtype)


def ada_mod(c, w, b):
    return (jax.nn.silu(c) @ w + b)[:, None, :]


def rope_tables(seq_len):
    pos = jnp.arange(seq_len, dtype=jnp.float32)
    inv_freq = ROPE_THETA ** (-jnp.arange(0, QK_ROPE_DIM, 2, dtype=jnp.float32) / QK_ROPE_DIM)
    ang = pos[:, None] * inv_freq[None, :]
    return jnp.cos(ang), jnp.sin(ang)


def apply_rope(x, cos, sin):
    xf = x.astype(jnp.float32)
    x1, x2 = jnp.split(xf, 2, axis=-1)
    return jnp.concatenate([x1 * cos - x2 * sin, x2 * cos + x1 * sin], axis=-1).astype(x.dtype)


def gmlp_mixer(h, w_in, ln_g, ln_b, w_s, b_s, w_out):
    B, S, _ = h.shape
    u, v, z = jnp.split(h @ w_in, 3, axis=-1)
    u = jax.nn.gelu(u)
    v = layer_norm(jax.nn.gelu(v), ln_g, ln_b)
    causal = jnp.tril(jnp.ones((CHUNK, CHUNK), dtype=bool))
    ws = jnp.where(causal, w_s, 0.0).astype(v.dtype)
    vc = v.reshape(B, S // CHUNK, CHUNK, GM_GROUPS, GM_GROUP_DIM)
    mixed = jnp.einsum('gts,bnsgc->bntgc', ws, vc) + b_s.T[None, None, :, :, None].astype(v.dtype)
    y = u * mixed.reshape(B, S, GM_WIDTH) * jax.nn.silu(z)
    return y @ w_out


def mla_shared_kv(h, w_dkv, g_kva, w_ukv, g_kn, g_kr, cos, sin):
    B, S, _ = h.shape
    c_kv, k_r = jnp.split(h @ w_dkv, [KV_LORA_RANK], axis=-1)
    c_kv = rms_norm(c_kv, g_kva)
    kv = (c_kv @ w_ukv).reshape(B, S, N_HEADS, QK_NOPE_DIM + V_HEAD_DIM)
    k_nope, v = jnp.split(kv, [QK_NOPE_DIM], axis=-1)
    k_nope = rms_norm(k_nope, g_kn)
    k_rope = apply_rope(rms_norm(k_r, g_kr), cos, sin)
    return k_nope, k_rope, v


def causal_block_attention(q_nope, q_rope, k_nope, k_rope, v):
    B, S, H, _ = q_nope.shape
    nb = S // Q_BLOCK
    scale = (QK_NOPE_DIM + QK_ROPE_DIM) ** -0.5
    qn = q_nope.reshape(B, nb, Q_BLOCK, H, QK_NOPE_DIM).transpose(1, 0, 2, 3, 4)
    qr = q_rope.reshape(B, nb, Q_BLOCK, H, QK_ROPE_DIM).transpose(1, 0, 2, 3, 4)
    k_pos = jnp.arange(S)

    def block(args):
        qn_b, qr_b, i = args
        s = (jnp.einsum('bqhd,bkhd->bhqk', qn_b, k_nope)
             + jnp.einsum('bqhr,bkr->bhqk', qr_b, k_rope)).astype(jnp.float32) * scale
        q_pos = i * Q_BLOCK + jnp.arange(Q_BLOCK)
        s = jnp.where(k_pos[None, :] <= q_pos[:, None], s, -jnp.inf)
        p = jax.nn.softmax(s, axis=-1).astype(v.dtype)
        return jnp.einsum('bhqk,bkhd->bqhd', p, v)

    o = lax.map(block, (qn, qr, jnp.arange(nb)))
    return o.transpose(1, 0, 2, 3, 4).reshape(B, S, H, V_HEAD_DIM)


def mla_mixer(h, k_nope, k_rope, v, w_in, g_qa, w_uq, g_qn, g_qr, w_out, cos, sin):
    B, S, _ = h.shape
    c_q, z = jnp.split(h @ w_in, [Q_LORA_RANK], axis=-1)
    q = (rms_norm(c_q, g_qa) @ w_uq).reshape(B, S, N_HEADS, QK_NOPE_DIM + QK_ROPE_DIM)
    q_nope, q_rope = jnp.split(q, [QK_NOPE_DIM], axis=-1)
    q_nope = rms_norm(q_nope, g_qn)
    q_rope = apply_rope(rms_norm(q_rope, g_qr), cos[:, None, :], sin[:, None, :])
    o = causal_block_attention(q_nope, q_rope, k_nope, k_rope, v)
    y = o.reshape(B, S, ATTN_WIDTH) * jax.nn.silu(z)
    return y @ w_out


def setup_inputs(seed: int = 0) -> dict:
    key = jax.random.key(seed)
    ks = iter(jax.random.split(key, 32))
    D = D_MODEL

    def nrm(shape, std):
        return std * jax.random.normal(next(ks), shape, jnp.float32)

    return {
        "x": nrm((BATCH, SEQ, D), 1.0),
        "c": nrm((BATCH, D), 1.0),
        "ada_w": nrm((DEPTH, D, 3 * D), 0.5 * D ** -0.5),
        "ada_b": nrm((DEPTH, 3 * D), 0.02),
        "norm_g": 1.0 + nrm((DEPTH, D), 0.02),
        "a_w_in": nrm((N_A_LAYERS, D, 3 * GM_WIDTH), D ** -0.5),
        "a_ln_g": 1.0 + nrm((N_A_LAYERS, GM_WIDTH), 0.02),
        "a_ln_b": nrm((N_A_LAYERS, GM_WIDTH), 0.02),
        "a_w_s": nrm((N_A_LAYERS, GM_GROUPS, CHUNK, CHUNK), CHUNK ** -0.5),
        "a_b_s": 1.0 + nrm((N_A_LAYERS, GM_GROUPS, CHUNK), 0.1),
        "a_w_out": nrm((N_A_LAYERS, GM_WIDTH, D), GM_WIDTH ** -0.5),
        "kv_ada_w": nrm((D, 2 * D), 0.5 * D ** -0.5),
        "kv_ada_b": nrm((2 * D,), 0.02),
        "kv_norm_g": 1.0 + nrm((D,), 0.02),
        "kv_w_dkv": nrm((D, KV_LORA_RANK + QK_ROPE_DIM), D ** -0.5),
        "kv_g_kva": 1.0 + nrm((KV_LORA_RANK,), 0.02),
        "kv_w_ukv": nrm((KV_LORA_RANK, N_HEADS * (QK_NOPE_DIM + V_HEAD_DIM)), KV_LORA_RANK ** -0.5),
        "kv_g_kn": 1.0 + nrm((QK_NOPE_DIM,), 0.02),
        "kv_g_kr": 1.0 + nrm((QK_ROPE_DIM,), 0.02),
        "b_w_in": nrm((N_B_LAYERS, D, Q_LORA_RANK + ATTN_WIDTH), D ** -0.5),
        "b_g_qa": 1.0 + nrm((N_B_LAYERS, Q_LORA_RANK), 0.02),
        "b_w_uq": nrm((N_B_LAYERS, Q_LORA_RANK, N_HEADS * (QK_NOPE_DIM + QK_ROPE_DIM)), Q_LORA_RANK ** -0.5),
        "b_g_qn": 1.0 + nrm((N_B_LAYERS, QK_NOPE_DIM), 0.02),
        "b_g_qr": 1.0 + nrm((N_B_LAYERS, QK_ROPE_DIM), 0.02),
        "b_w_out": nrm((N_B_LAYERS, ATTN_WIDTH, D), ATTN_WIDTH ** -0.5),
    }


def reference(x, c, ada_w, ada_b, norm_g, a_w_in, a_ln_g, a_ln_b, a_w_s, a_b_s, a_w_out,
              kv_ada_w, kv_ada_b, kv_norm_g, kv_w_dkv, kv_g_kva, kv_w_ukv, kv_g_kn, kv_g_kr,
              b_w_in, b_g_qa, b_w_uq, b_g_qn, b_g_qr, b_w_out):
    cos, sin = rope_tables(x.shape[1])
    k_nope = k_rope = v = None
    for i in range(DEPTH):
        if i == N_A_LAYERS:
            kv_shift, kv_scale = jnp.split(ada_mod(c, kv_ada_w, kv_ada_b), 2, axis=-1)
            h_kv = rms_norm(x, kv_norm_g) * (1.0 + kv_scale) + kv_shift
            k_nope, k_rope, v = mla_shared_kv(h_kv, kv_w_dkv, kv_g_kva, kv_w_ukv, kv_g_kn, kv_g_kr, cos, sin)
        shift, scale, gate = jnp.split(ada_mod(c, ada_w[i], ada_b[i]), 3, axis=-1)
        h = rms_norm(x, norm_g[i]) * (1.0 + scale) + shift
        if i < N_A_LAYERS:
            out = gmlp_mixer(h, a_w_in[i], a_ln_g[i], a_ln_b[i], a_w_s[i], a_b_s[i], a_w_out[i])
        else:
            j = i - N_A_LAYERS
            out = mla_mixer(h, k_nope, k_rope, v, b_w_in[j], b_g_qa[j], b_w_uq[j], b_g_qn[j],
                            b_g_qr[j], b_w_out[j], cos, sin)
        x = x + gate * out
    return x
```

```python
import functools
import math

import jax
import jax.numpy as jnp
from jax import lax
from jax.experimental import pallas as pl
from jax.experimental.pallas import tpu as pltpu

F32 = jnp.float32
BF16 = jnp.bfloat16

D_MODEL = 2048
N_A_LAYERS = 2
N_B_LAYERS = 2
DEPTH = N_A_LAYERS + N_B_LAYERS
GM_WIDTH = 2 * D_MODEL
GM_GROUPS = 16
GM_GROUP_DIM = GM_WIDTH // GM_GROUPS
CHUNK = 128
N_HEADS = 16
QK_NOPE_DIM = 128
QK_ROPE_DIM = 64
HALF_ROPE = QK_ROPE_DIM // 2
QK_DIM = QK_NOPE_DIM + QK_ROPE_DIM
V_HEAD_DIM = 128
Q_LORA_RANK = 512
KV_LORA_RANK = 512
ATTN_WIDTH = N_HEADS * V_HEAD_DIM
ROPE_THETA = 10000.0
EPS = 1e-6

LANES = 128
HEAD_SLAB = 2 * LANES
VMEM_LIMIT = 56 * 1024 * 1024

GELU_C = math.sqrt(2.0 / math.pi)


def _silu(a):
    return a / (1.0 + jnp.exp(-a))


def _gelu_tanh(a):
    return 0.5 * a * (1.0 + jnp.tanh(GELU_C * (a + 0.044715 * (a * a * a))))


def _rms(a, width):
    return a * lax.rsqrt(jnp.sum(a * a, axis=-1, keepdims=True) * (1.0 / width) + EPS)


def _ada_norm(x, g, scale, shift):
    return (_rms(x, x.shape[-1]) * (g * (1.0 + scale)) + shift).astype(BF16)


def _rope(r, cos, sin_hi, sin_lo):
    return (r * cos + pltpu.roll(r, HALF_ROPE, 1) * sin_hi
            + pltpu.roll(r, LANES - HALF_ROPE, 1) * sin_lo)


def _params(n_axes):
    return pltpu.CompilerParams(dimension_semantics=("arbitrary",) * n_axes,
                                vmem_limit_bytes=VMEM_LIMIT)


def _resident(shape):
    return pl.BlockSpec(shape, lambda *_: (0,) * len(shape), pipeline_mode=pl.Buffered(1))


def _ada_kernel(c_ref, w_ref, b_ref, o_ref):
    a = _silu(c_ref[...]).astype(BF16)
    o_ref[...] = jnp.dot(a, w_ref[...].astype(BF16), preferred_element_type=F32) + b_ref[...]


def _ada_mod(c_pad, w, b, tn=1024):
    n_layers, d, n = w.shape
    return pl.pallas_call(
        _ada_kernel,
        grid=(n_layers, n // tn),
        in_specs=[pl.BlockSpec(c_pad.shape, lambda l, j: (0, 0)),
                  pl.BlockSpec((None, d, tn), lambda l, j: (l, 0, j)),
                  pl.BlockSpec((None, 1, tn), lambda l, j: (l, 0, j))],
        out_specs=pl.BlockSpec((None, c_pad.shape[0], tn), lambda l, j: (l, 0, j)),
        out_shape=jax.ShapeDtypeStruct((n_layers, c_pad.shape[0], n), F32),
        compiler_params=_params(2),
        name="ada_mod",
    )(c_pad, w, b.reshape(n_layers, 1, n))


def _gmlp_in_kernel(x_ref, g_ref, sc_ref, sh_ref, w_ref, o_ref, h_ref, *, n_gelu_tiles):
    j = pl.program_id(1)

    @pl.when(j == 0)
    def _():
        h_ref[...] = _ada_norm(x_ref[...], g_ref[...], sc_ref[...], sh_ref[...])

    a = jnp.dot(h_ref[...], w_ref[...], preferred_element_type=F32)

    @pl.when(j < n_gelu_tiles)
    def _():
        o_ref[...] = _gelu_tanh(a).astype(BF16)

    @pl.when(j >= n_gelu_tiles)
    def _():
        o_ref[...] = _silu(a).astype(BF16)


def _gmlp_in(x, g, scale, shift, w, seq, tm=512, tn=1024):
    t, d = x.shape
    n = w.shape[1]
    per_batch = seq // tm
    vec = pl.BlockSpec((None, 1, d), lambda i, j: (i // per_batch, 0, 0))
    return pl.pallas_call(
        functools.partial(_gmlp_in_kernel, n_gelu_tiles=2 * GM_WIDTH // tn),
        grid=(t // tm, n // tn),
        in_specs=[pl.BlockSpec((tm, d), lambda i, j: (i, 0)),
                  pl.BlockSpec((1, d), lambda i, j: (0, 0)),
                  vec, vec,
                  pl.BlockSpec((d, tn), lambda i, j: (0, j))],
        out_specs=pl.BlockSpec((tm, tn), lambda i, j: (i, j)),
        out_shape=jax.ShapeDtypeStruct((t, n), BF16),
        scratch_shapes=[pltpu.VMEM((tm, d), BF16)],
        compiler_params=_params(2),
        name="gmlp_in",
    )(x, g, scale, shift, w)


def _gmlp_out_kernel(x_ref, u_ref, v_ref, z_ref, lng_ref, lnb_ref, ws_ref, bst_ref, w_ref,
                     gate_ref, o_ref, wsm_ref, vn_ref, y_ref):
    @pl.when(pl.program_id(0) == 0)
    def _():
        row = lax.broadcasted_iota(jnp.int32, (CHUNK, CHUNK), 0)
        col = lax.broadcasted_iota(jnp.int32, (CHUNK, CHUNK), 1)
        for g in range(GM_GROUPS):
            wsm_ref[g] = jnp.where(col <= row, ws_ref[g], 0.0).astype(BF16)

    v = v_ref[...].astype(F32)
    mu = jnp.mean(v, axis=-1, keepdims=True)
    vc = v - mu
    var = jnp.mean(vc * vc, axis=-1, keepdims=True)
    vn_ref[...] = (vc * lax.rsqrt(var + EPS) * lng_ref[...] + lnb_ref[...]).astype(BF16)

    tm = x_ref.shape[0]
    for c in range(tm // CHUNK):
        rows = pl.ds(c * CHUNK, CHUNK)
        for g in range(GM_GROUPS):
            cols = pl.ds(g * GM_GROUP_DIM, GM_GROUP_DIM)
            mixed = jnp.dot(wsm_ref[g], vn_ref[rows, cols], preferred_element_type=F32)
            mixed = mixed + bst_ref[:, g:g + 1]
            y = u_ref[rows, cols].astype(F32) * mixed * z_ref[rows, cols].astype(F32)
            y_ref[rows, cols] = y.astype(BF16)

    out = jnp.dot(y_ref[...], w_ref[...], preferred_element_type=F32)
    o_ref[...] = x_ref[...] + gate_ref[...] * out


def _gmlp_out(x, uvz, ln_g, ln_b, w_s, b_s_t, w_out, gate, seq, tm=256):
    t, d = x.shape
    per_batch = seq // tm
    part = lambda k: pl.BlockSpec((tm, GM_WIDTH), lambda i: (i, k))
    return pl.pallas_call(
        _gmlp_out_kernel,
        grid=(t // tm,),
        in_specs=[pl.BlockSpec((tm, d), lambda i: (i, 0)),
                  part(0), part(1), part(2),
                  _resident((1, GM_WIDTH)), _resident((1, GM_WIDTH)),
                  _resident(w_s.shape), _resident(b_s_t.shape), _resident(w_out.shape),
                  pl.BlockSpec((None, 1, d), lambda i: (i // per_batch, 0, 0))],
        out_specs=pl.BlockSpec((tm, d), lambda i: (i, 0)),
        out_shape=jax.ShapeDtypeStruct((t, d), F32),
        scratch_shapes=[pltpu.VMEM((GM_GROUPS, CHUNK, CHUNK), BF16),
                        pltpu.VMEM((tm, GM_WIDTH), BF16),
                        pltpu.VMEM((tm, GM_WIDTH), BF16)],
        compiler_params=_params(1),
        name="gmlp_out",
    )(x, uvz, uvz, uvz, ln_g, ln_b, w_s, b_s_t, w_out, gate)


def _kv_kernel(x_ref, g_ref, sc_ref, sh_ref, wd_ref, gkva_ref, gkr_ref, wu_ref, gkn_ref,
               cos_ref, shi_ref, slo_ref, k_ref, v_ref):
    h = _ada_norm(x_ref[...], g_ref[...], sc_ref[...], sh_ref[...])
    ckr = jnp.dot(h, wd_ref[...], preferred_element_type=F32)
    c_kv = (_rms(ckr[:, :KV_LORA_RANK], KV_LORA_RANK) * gkva_ref[...]).astype(BF16)
    k_r = _rms(ckr[:, KV_LORA_RANK:], QK_ROPE_DIM) * gkr_ref[...]
    k_r = _rope(k_r, cos_ref[...], shi_ref[...], slo_ref[...])[:, :QK_ROPE_DIM].astype(BF16)
    kv = jnp.dot(c_kv, wu_ref[...], preferred_element_type=F32)
    for hd in range(N_HEADS):
        base = hd * HEAD_SLAB
        k_n = _rms(kv[:, base:base + QK_NOPE_DIM], QK_NOPE_DIM) * gkn_ref[...]
        k_ref[hd, :, :QK_NOPE_DIM] = k_n.astype(BF16)
        k_ref[hd, :, QK_NOPE_DIM:] = k_r
        v_ref[hd] = kv[:, base + QK_NOPE_DIM:base + HEAD_SLAB].astype(BF16)


def _kv_proj(x, g, scale, shift, wd, gkva, gkr, wu, gkn, cos, s_hi, s_lo, batch, seq, tm=512):
    t, d = x.shape
    per_batch = seq // tm
    vec = pl.BlockSpec((None, 1, d), lambda i: (i // per_batch, 0, 0))
    tab = pl.BlockSpec((tm, LANES), lambda i: (i % per_batch, 0))
    head_out = lambda w: pl.BlockSpec((None, N_HEADS, tm, w),
                                      lambda i: (i // per_batch, 0, i % per_batch, 0))
    return pl.pallas_call(
        _kv_kernel,
        grid=(t // tm,),
        in_specs=[pl.BlockSpec((tm, d), lambda i: (i, 0)),
                  _resident((1, d)), vec, vec,
                  _resident(wd.shape), _resident(gkva.shape), _resident(gkr.shape),
                  _resident(wu.shape), _resident(gkn.shape), tab, tab, tab],
        out_specs=[head_out(QK_DIM), head_out(V_HEAD_DIM)],
        out_shape=[jax.ShapeDtypeStruct((batch, N_HEADS, seq, QK_DIM), BF16),
                   jax.ShapeDtypeStruct((batch, N_HEADS, seq, V_HEAD_DIM), BF16)],
        compiler_params=_params(1),
        name="kv_proj",
    )(x, g, scale, shift, wd, gkva, gkr, wu, gkn, cos, s_hi, s_lo)


def _mla_in_kernel(x_ref, g_ref, sc_ref, sh_ref, win_ref, gqa_ref, wuq_ref, gqn_ref, gqr_ref,
                   cos_ref, shi_ref, slo_ref, q_ref, z_ref):
    h = _ada_norm(x_ref[...], g_ref[...], sc_ref[...], sh_ref[...])
    cz = jnp.dot(h, win_ref[...], preferred_element_type=F32)
    z_ref[...] = _silu(cz[:, Q_LORA_RANK:]).astype(BF16)
    c_q = (_rms(cz[:, :Q_LORA_RANK], Q_LORA_RANK) * gqa_ref[...]).astype(BF16)
    q = jnp.dot(c_q, wuq_ref[...], preferred_element_type=F32)
    cos, s_hi, s_lo = cos_ref[...], shi_ref[...], slo_ref[...]
    for hd in range(N_HEADS):
        base = hd * HEAD_SLAB
        q_n = _rms(q[:, base:base + QK_NOPE_DIM], QK_NOPE_DIM) * gqn_ref[...]
        q_r = _rms(q[:, base + QK_NOPE_DIM:base + HEAD_SLAB], QK_ROPE_DIM) * gqr_ref[...]
        q_r = _rope(q_r, cos, s_hi, s_lo)
        q_ref[hd, :, :QK_NOPE_DIM] = q_n.astype(BF16)
        q_ref[hd, :, QK_NOPE_DIM:] = q_r[:, :QK_ROPE_DIM].astype(BF16)


def _mla_in(x, g, scale, shift, w_in, gqa, wuq, gqn, gqr, cos, s_hi, s_lo, batch, seq, tm=512):
    t, d = x.shape
    per_batch = seq // tm
    vec = pl.BlockSpec((None, 1, d), lambda i: (i // per_batch, 0, 0))
    tab = pl.BlockSpec((tm, LANES), lambda i: (i % per_batch, 0))
    return pl.pallas_call(
        _mla_in_kernel,
        grid=(t // tm,),
        in_specs=[pl.BlockSpec((tm, d), lambda i: (i, 0)),
                  _resident((1, d)), vec, vec,
                  _resident(w_in.shape), _resident(gqa.shape), _resident(wuq.shape),
                  _resident(gqn.shape), _resident(gqr.shape), tab, tab, tab],
        out_specs=[pl.BlockSpec((None, N_HEADS, tm, QK_DIM),
                                lambda i: (i // per_batch, 0, i % per_batch, 0)),
                   pl.BlockSpec((tm, ATTN_WIDTH), lambda i: (i, 0))],
        out_shape=[jax.ShapeDtypeStruct((batch, N_HEADS, seq, QK_DIM), BF16),
                   jax.ShapeDtypeStruct((t, ATTN_WIDTH), BF16)],
        compiler_params=_params(1),
        name="mla_in",
    )(x, g, scale, shift, w_in, gqa, wuq, gqn, gqr, cos, s_hi, s_lo)


def _attn_kernel(q_ref, k_ref, v_ref, o_ref, *, tile):
    qi = pl.program_id(2)
    q = q_ref[...]

    def step(j, carry, diagonal):
        m, l, acc = carry
        start = pl.multiple_of(j * tile, tile)
        k = k_ref[pl.ds(start, tile), :]
        v = v_ref[pl.ds(start, tile), :]
        s = lax.dot_general(q, k, (((1,), (1,)), ((), ())), preferred_element_type=F32)
        if diagonal:
            row = lax.broadcasted_iota(jnp.int32, s.shape, 0)
            col = lax.broadcasted_iota(jnp.int32, s.shape, 1)
            s = jnp.where(col <= row, s, -jnp.inf)
        m_new = jnp.maximum(m, jnp.max(s, axis=-1, keepdims=True))
        alpha = jnp.exp(m - m_new)
        p = jnp.exp(s - m_new)
        l = alpha * l + jnp.sum(p, axis=-1, keepdims=True)
        acc = alpha * acc + jnp.dot(p.astype(BF16), v, preferred_element_type=F32)
        return m_new, l, acc

    init = (jnp.full((tile, 1), -jnp.inf, F32), jnp.zeros((tile, 1), F32),
            jnp.zeros((tile, V_HEAD_DIM), F32))
    carry = lax.fori_loop(0, qi, lambda j, c: step(j, c, False), init)
    _, l, acc = step(qi, carry, True)
    o_ref[...] = (acc / l).astype(o_ref.dtype)


def _attention(q, k, v, tile=512):
    batch, heads, seq, _ = q.shape
    return pl.pallas_call(
        functools.partial(_attn_kernel, tile=tile),
        grid=(batch, heads, seq // tile),
        in_specs=[pl.BlockSpec((None, None, tile, QK_DIM), lambda b, h, i: (b, h, i, 0)),
                  pl.BlockSpec((None, None, seq, QK_DIM), lambda b, h, i: (b, h, 0, 0)),
                  pl.BlockSpec((None, None, seq, V_HEAD_DIM), lambda b, h, i: (b, h, 0, 0))],
        out_specs=pl.BlockSpec((None, tile, V_HEAD_DIM), lambda b, h, i: (b, i, h)),
        out_shape=jax.ShapeDtypeStruct((batch, seq, heads * V_HEAD_DIM), BF16),
        compiler_params=_params(3),
        name="attention",
    )(q, k, v)


def _mla_out_kernel(x_ref, a_ref, z_ref, w_ref, gate_ref, o_ref):
    y = (a_ref[...].astype(F32) * z_ref[...].astype(F32)).astype(BF16)
    out = jnp.dot(y, w_ref[...], preferred_element_type=F32)
    o_ref[...] = x_ref[...] + gate_ref[...] * out


def _mla_out(x, a, z, w_out, gate, seq, tm=512):
    t, d = x.shape
    per_batch = seq // tm
    row = lambda w: pl.BlockSpec((tm, w), lambda i: (i, 0))
    return pl.pallas_call(
        _mla_out_kernel,
        grid=(t // tm,),
        in_specs=[row(d), row(ATTN_WIDTH), row(ATTN_WIDTH), _resident(w_out.shape),
                  pl.BlockSpec((None, 1, d), lambda i: (i // per_batch, 0, 0))],
        out_specs=row(d),
        out_shape=jax.ShapeDtypeStruct((t, d), F32),
        compiler_params=_params(1),
        name="mla_out",
    )(x, a, z, w_out, gate)


def _pad_head_columns(w, head_dim):
    k = w.shape[0]
    w = w.reshape(k, N_HEADS, head_dim)
    w = jnp.pad(w, ((0, 0), (0, 0), (0, HEAD_SLAB - head_dim)))
    return w.reshape(k, N_HEADS * HEAD_SLAB)


def _rope_tables(seq):
    pos = jnp.arange(seq, dtype=F32)
    inv_freq = ROPE_THETA ** (-jnp.arange(0, QK_ROPE_DIM, 2, dtype=F32) / QK_ROPE_DIM)
    ang = pos[:, None] * inv_freq[None, :]
    cos, sin = jnp.cos(ang), jnp.sin(ang)
    zero = jnp.zeros_like(cos)
    pad = jnp.zeros((seq, LANES - QK_ROPE_DIM), F32)
    return (jnp.concatenate([cos, cos, pad], axis=1),
            jnp.concatenate([zero, sin, pad], axis=1),
            jnp.concatenate([-sin, zero, pad], axis=1))


def _pad_lanes(g):
    return jnp.pad(g, (0, LANES - g.shape[0])).reshape(1, LANES)


def kernel(x, c, ada_w, ada_b, norm_g, a_w_in, a_ln_g, a_ln_b, a_w_s, a_b_s, a_w_out, kv_ada_w, kv_ada_b, kv_norm_g, kv_w_dkv, kv_g_kva, kv_w_ukv, kv_g_kn, kv_g_kr, b_w_in, b_g_qa, b_w_uq, b_g_qn, b_g_qr, b_w_out):
    batch, seq, d = x.shape
    t = batch * seq
    sm_scale = QK_DIM ** -0.5

    c_pad = jnp.pad(c, ((0, 8 - batch), (0, 0)))
    mod = _ada_mod(c_pad, ada_w, ada_b)[:, :batch]
    kv_mod = _ada_mod(c_pad, kv_ada_w[None], kv_ada_b[None])[0, :batch]
    vecs = lambda m, n: [v.reshape(batch, 1, d) for v in jnp.split(m, n, axis=-1)]

    cos, s_hi, s_lo = _rope_tables(seq)
    xf = x.reshape(t, d)

    for i in range(N_A_LAYERS):
        shift, scale, gate = vecs(mod[i], 3)
        uvz = _gmlp_in(xf, norm_g[i].reshape(1, d), scale, shift, a_w_in[i].astype(BF16), seq)
        xf = _gmlp_out(xf, uvz, a_ln_g[i].reshape(1, GM_WIDTH), a_ln_b[i].reshape(1, GM_WIDTH),
                       a_w_s[i], a_b_s[i].T, a_w_out[i].astype(BF16), gate, seq)

    kv_shift, kv_scale = vecs(kv_mod, 2)
    wd = jnp.pad(kv_w_dkv, ((0, 0), (0, LANES - QK_ROPE_DIM))).astype(BF16)
    k_all, v_all = _kv_proj(xf, kv_norm_g.reshape(1, d), kv_scale, kv_shift, wd,
                            kv_g_kva.reshape(1, KV_LORA_RANK), _pad_lanes(kv_g_kr),
                            kv_w_ukv.astype(BF16), kv_g_kn.reshape(1, QK_NOPE_DIM),
                            cos, s_hi, s_lo, batch, seq)

    for j in range(N_B_LAYERS):
        shift, scale, gate = vecs(mod[N_A_LAYERS + j], 3)
        q, z = _mla_in(xf, norm_g[N_A_LAYERS + j].reshape(1, d), scale, shift,
                       b_w_in[j].astype(BF16), b_g_qa[j].reshape(1, Q_LORA_RANK),
                       _pad_head_columns(b_w_uq[j], QK_DIM).astype(BF16),
                       (b_g_qn[j] * sm_scale).reshape(1, QK_NOPE_DIM),
                       _pad_lanes(b_g_qr[j] * sm_scale), cos, s_hi, s_lo, batch, seq)
        a = _attention(q, k_all, v_all)
        xf = _mla_out(xf, a.reshape(t, ATTN_WIDTH), z, b_w_out[j].astype(BF16), gate, seq)

    return xf.reshape(batch, seq, d)
```

```python
import functools
import math

import jax
import jax.numpy as jnp
from jax import lax
from jax.experimental import pallas as pl
from jax.experimental.pallas import tpu as pltpu

F32 = jnp.float32
BF16 = jnp.bfloat16

D_MODEL = 2048
N_A_LAYERS = 2
N_B_LAYERS = 2
DEPTH = N_A_LAYERS + N_B_LAYERS
GM_WIDTH = 2 * D_MODEL
GM_GROUPS = 16
GM_GROUP_DIM = GM_WIDTH // GM_GROUPS
CHUNK = 128
N_HEADS = 16
QK_NOPE_DIM = 128
QK_ROPE_DIM = 64
HALF_ROPE = QK_ROPE_DIM // 2
QK_DIM = QK_NOPE_DIM + QK_ROPE_DIM
V_HEAD_DIM = 128
Q_LORA_RANK = 512
KV_LORA_RANK = 512
ATTN_WIDTH = N_HEADS * V_HEAD_DIM
ROPE_THETA = 10000.0
EPS = 1e-6

LANES = 128
HEAD_SLAB = 2 * LANES
VMEM_LIMIT = 56 * 1024 * 1024

GELU_C = math.sqrt(2.0 / math.pi)


def _silu(a):
    return a / (1.0 + jnp.exp(-a))


def _rms(a, width):
    return a * lax.rsqrt(jnp.sum(a * a, axis=-1, keepdims=True) * (1.0 / width) + EPS)


def _ada_norm(x, g, scale, shift):
    return (_rms(x, x.shape[-1]) * (g * (1.0 + scale)) + shift).astype(BF16)


def _rope(r, cos, sin_hi, sin_lo):
    return (r * cos + pltpu.roll(r, HALF_ROPE, 1) * sin_hi
            + pltpu.roll(r, LANES - HALF_ROPE, 1) * sin_lo)


def _params(n_axes):
    return pltpu.CompilerParams(dimension_semantics=("arbitrary",) * n_axes,
                                vmem_limit_bytes=VMEM_LIMIT)


def _resident(shape):
    return pl.BlockSpec(shape, lambda *_: (0,) * len(shape), pipeline_mode=pl.Buffered(1))


def _ada_kernel(c_ref, w_ref, b_ref, o_ref):
    a = _silu(c_ref[...]).astype(BF16)
    o_ref[...] = jnp.dot(a, w_ref[...].astype(BF16), preferred_element_type=F32) + b_ref[...]


def _ada_mod(c_pad, w, b, tn=1024):
    n_layers, d, n = w.shape
    return pl.pallas_call(
        _ada_kernel,
        grid=(n_layers, n // tn),
        in_specs=[pl.BlockSpec(c_pad.shape, lambda l, j: (0, 0)),
                  pl.BlockSpec((None, d, tn), lambda l, j: (l, 0, j)),
                  pl.BlockSpec((None, 1, tn), lambda l, j: (l, 0, j))],
        out_specs=pl.BlockSpec((None, c_pad.shape[0], tn), lambda l, j: (l, 0, j)),
        out_shape=jax.ShapeDtypeStruct((n_layers, c_pad.shape[0], n), F32),
        compiler_params=_params(2),
        name="ada_mod",
    )(c_pad, w, b.reshape(n_layers, 1, n))


def _gmlp_in_kernel(x_ref, g_ref, sc_ref, sh_ref, w_ref, o_ref, h_ref, *, n_gelu_tiles, chunk):
    j = pl.program_id(1)

    @pl.when(j == 0)
    def _():
        h_ref[...] = _ada_norm(x_ref[...], g_ref[...], sc_ref[...], sh_ref[...])

    is_gelu = j < n_gelu_tiles
    c1 = jnp.where(is_gelu, 2.0 * GELU_C, 1.0).astype(F32)
    c3 = jnp.where(is_gelu, 2.0 * GELU_C * 0.044715, 0.0).astype(F32)
    h = h_ref[...]
    for n0 in range(0, o_ref.shape[1], chunk):
        a = jnp.dot(h, w_ref[:, n0:n0 + chunk], preferred_element_type=F32)
        gate = a * (c1 + c3 * (a * a))
        o_ref[:, n0:n0 + chunk] = (a / (1.0 + jnp.exp(-gate))).astype(BF16)


def _gmlp_in(x, g, scale, shift, w, seq, tm=512, tn=2048, chunk=512):
    t, d = x.shape
    n = w.shape[1]
    per_batch = seq // tm
    vec = pl.BlockSpec((None, 1, d), lambda i, j: (i // per_batch, 0, 0))
    return pl.pallas_call(
        functools.partial(_gmlp_in_kernel, n_gelu_tiles=2 * GM_WIDTH // tn, chunk=chunk),
        grid=(t // tm, n // tn),
        in_specs=[pl.BlockSpec((tm, d), lambda i, j: (i, 0)),
                  pl.BlockSpec((1, d), lambda i, j: (0, 0)),
                  vec, vec,
                  pl.BlockSpec((d, tn), lambda i, j: (0, j))],
        out_specs=pl.BlockSpec((tm, tn), lambda i, j: (i, j)),
        out_shape=jax.ShapeDtypeStruct((t, n), BF16),
        scratch_shapes=[pltpu.VMEM((tm, d), BF16)],
        compiler_params=_params(2),
        name="gmlp_in",
    )(x, g, scale, shift, w)


def _gmlp_out_kernel(x_ref, u_ref, v_ref, z_ref, lng_ref, lnb_ref, ws_ref, bst_ref, w_ref,
                     gate_ref, o_ref, wsm_ref, vn_ref, y_ref):
    @pl.when(pl.program_id(0) == 0)
    def _():
        row = lax.broadcasted_iota(jnp.int32, (CHUNK, CHUNK), 0)
        col = lax.broadcasted_iota(jnp.int32, (CHUNK, CHUNK), 1)
        for g in range(GM_GROUPS):
            wsm_ref[g] = jnp.where(col <= row, ws_ref[g], 0.0).astype(BF16)

    v = v_ref[...].astype(F32)
    mu = jnp.mean(v, axis=-1, keepdims=True)
    vc = v - mu
    var = jnp.mean(vc * vc, axis=-1, keepdims=True)
    vn_ref[...] = (vc * lax.rsqrt(var + EPS) * lng_ref[...] + lnb_ref[...]).astype(BF16)

    tm = x_ref.shape[0]
    for c in range(tm // CHUNK):
        rows = pl.ds(c * CHUNK, CHUNK)
        for g in range(GM_GROUPS):
            cols = pl.ds(g * GM_GROUP_DIM, GM_GROUP_DIM)
            mixed = jnp.dot(wsm_ref[g], vn_ref[rows, cols], preferred_element_type=F32)
            mixed = mixed + bst_ref[:, g:g + 1]
            y = u_ref[rows, cols].astype(F32) * mixed * z_ref[rows, cols].astype(F32)
            y_ref[rows, cols] = y.astype(BF16)

    out = jnp.dot(y_ref[...], w_ref[...], preferred_element_type=F32)
    o_ref[...] = x_ref[...] + gate_ref[...] * out


def _gmlp_out(x, uvz, ln_g, ln_b, w_s, b_s_t, w_out, gate, seq, tm=256):
    t, d = x.shape
    per_batch = seq // tm
    part = lambda k: pl.BlockSpec((tm, GM_WIDTH), lambda i: (i, k))
    return pl.pallas_call(
        _gmlp_out_kernel,
        grid=(t // tm,),
        in_specs=[pl.BlockSpec((tm, d), lambda i: (i, 0)),
                  part(0), part(1), part(2),
                  _resident((1, GM_WIDTH)), _resident((1, GM_WIDTH)),
                  _resident(w_s.shape), _resident(b_s_t.shape), _resident(w_out.shape),
                  pl.BlockSpec((None, 1, d), lambda i: (i // per_batch, 0, 0))],
        out_specs=pl.BlockSpec((tm, d), lambda i: (i, 0)),
        out_shape=jax.ShapeDtypeStruct((t, d), F32),
        scratch_shapes=[pltpu.VMEM((GM_GROUPS, CHUNK, CHUNK), BF16),
                        pltpu.VMEM((tm, GM_WIDTH), BF16),
                        pltpu.VMEM((tm, GM_WIDTH), BF16)],
        compiler_params=_params(1),
        name="gmlp_out",
    )(x, uvz, uvz, uvz, ln_g, ln_b, w_s, b_s_t, w_out, gate)


def _kv_kernel(x_ref, g_ref, sc_ref, sh_ref, wd_ref, gkva_ref, gkr_ref, wu_ref, gkn_ref,
               cos_ref, shi_ref, slo_ref, k_ref, v_ref):
    h = _ada_norm(x_ref[...], g_ref[...], sc_ref[...], sh_ref[...])
    ckr = jnp.dot(h, wd_ref[...], preferred_element_type=F32)
    c_kv = (_rms(ckr[:, :KV_LORA_RANK], KV_LORA_RANK) * gkva_ref[...]).astype(BF16)
    k_r = _rms(ckr[:, KV_LORA_RANK:], QK_ROPE_DIM) * gkr_ref[...]
    k_r = _rope(k_r, cos_ref[...], shi_ref[...], slo_ref[...])[:, :QK_ROPE_DIM].astype(BF16)
    kv = jnp.dot(c_kv, wu_ref[...], preferred_element_type=F32)
    for hd in range(N_HEADS):
        base = hd * HEAD_SLAB
        k_n = _rms(kv[:, base:base + QK_NOPE_DIM], QK_NOPE_DIM) * gkn_ref[...]
        k_ref[hd, :, :QK_NOPE_DIM] = k_n.astype(BF16)
        k_ref[hd, :, QK_NOPE_DIM:] = k_r
        v_ref[hd] = kv[:, base + QK_NOPE_DIM:base + HEAD_SLAB].astype(BF16)


def _kv_proj(x, g, scale, shift, wd, gkva, gkr, wu, gkn, cos, s_hi, s_lo, batch, seq, tm=512):
    t, d = x.shape
    per_batch = seq // tm
    vec = pl.BlockSpec((None, 1, d), lambda i: (i // per_batch, 0, 0))
    tab = pl.BlockSpec((tm, LANES), lambda i: (i % per_batch, 0))
    head_out = lambda w: pl.BlockSpec((None, N_HEADS, tm, w),
                                      lambda i: (i // per_batch, 0, i % per_batch, 0))
    return pl.pallas_call(
        _kv_kernel,
        grid=(t // tm,),
        in_specs=[pl.BlockSpec((tm, d), lambda i: (i, 0)),
                  _resident((1, d)), vec, vec,
                  _resident(wd.shape), _resident(gkva.shape), _resident(gkr.shape),
                  _resident(wu.shape), _resident(gkn.shape), tab, tab, tab],
        out_specs=[head_out(QK_DIM), head_out(V_HEAD_DIM)],
        out_shape=[jax.ShapeDtypeStruct((batch, N_HEADS, seq, QK_DIM), BF16),
                   jax.ShapeDtypeStruct((batch, N_HEADS, seq, V_HEAD_DIM), BF16)],
        compiler_params=_params(1),
        name="kv_proj",
    )(x, g, scale, shift, wd, gkva, gkr, wu, gkn, cos, s_hi, s_lo)


def _mla_in_kernel(x_ref, g_ref, sc_ref, sh_ref, win_ref, gqa_ref, wuq_ref, gqn_ref, gqr_ref,
                   cos_ref, shi_ref, slo_ref, q_ref, z_ref):
    h = _ada_norm(x_ref[...], g_ref[...], sc_ref[...], sh_ref[...])
    cz = jnp.dot(h, win_ref[...], preferred_element_type=F32)
    z_ref[...] = _silu(cz[:, Q_LORA_RANK:]).astype(BF16)
    c_q = (_rms(cz[:, :Q_LORA_RANK], Q_LORA_RANK) * gqa_ref[...]).astype(BF16)
    q = jnp.dot(c_q, wuq_ref[...], preferred_element_type=F32)
    cos, s_hi, s_lo = cos_ref[...], shi_ref[...], slo_ref[...]
    for hd in range(N_HEADS):
        base = hd * HEAD_SLAB
        q_n = _rms(q[:, base:base + QK_NOPE_DIM], QK_NOPE_DIM) * gqn_ref[...]
        q_r = _rms(q[:, base + QK_NOPE_DIM:base + HEAD_SLAB], QK_ROPE_DIM) * gqr_ref[...]
        q_r = _rope(q_r, cos, s_hi, s_lo)
        q_ref[hd, :, :QK_NOPE_DIM] = q_n.astype(BF16)
        q_ref[hd, :, QK_NOPE_DIM:] = q_r[:, :QK_ROPE_DIM].astype(BF16)


def _mla_in(x, g, scale, shift, w_in, gqa, wuq, gqn, gqr, cos, s_hi, s_lo, batch, seq, tm=512):
    t, d = x.shape
    per_batch = seq // tm
    vec = pl.BlockSpec((None, 1, d), lambda i: (i // per_batch, 0, 0))
    tab = pl.BlockSpec((tm, LANES), lambda i: (i % per_batch, 0))
    return pl.pallas_call(
        _mla_in_kernel,
        grid=(t // tm,),
        in_specs=[pl.BlockSpec((tm, d), lambda i: (i, 0)),
                  _resident((1, d)), vec, vec,
                  _resident(w_in.shape), _resident(gqa.shape), _resident(wuq.shape),
                  _resident(gqn.shape), _resident(gqr.shape), tab, tab, tab],
        out_specs=[pl.BlockSpec((None, N_HEADS, tm, QK_DIM),
                                lambda i: (i // per_batch, 0, i % per_batch, 0)),
                   pl.BlockSpec((tm, ATTN_WIDTH), lambda i: (i, 0))],
        out_shape=[jax.ShapeDtypeStruct((batch, N_HEADS, seq, QK_DIM), BF16),
                   jax.ShapeDtypeStruct((t, ATTN_WIDTH), BF16)],
        compiler_params=_params(1),
        name="mla_in",
    )(x, g, scale, shift, w_in, gqa, wuq, gqn, gqr, cos, s_hi, s_lo)


def _attn_kernel(q_ref, k_ref, v_ref, o_ref, sa_ref, sb_ref, m_ref, l_ref, acc_ref, *, tile):
    qi = pl.program_id(2)
    q = q_ref[...]

    def scores(j):
        k = k_ref[pl.ds(pl.multiple_of(j * tile, tile), tile), :]
        return lax.dot_general(q, k, (((1,), (1,)), ((), ())), preferred_element_type=F32)

    def update(j, s_ref, diagonal=False):
        v = v_ref[pl.ds(pl.multiple_of(j * tile, tile), tile), :]
        s = s_ref[...]
        if diagonal:
            row = lax.broadcasted_iota(jnp.int32, s.shape, 0)
            col = lax.broadcasted_iota(jnp.int32, s.shape, 1)
            s = jnp.where(col <= row, s, -jnp.inf)
        m = m_ref[...]
        m_new = jnp.maximum(m, jnp.max(s, axis=-1, keepdims=True))
        alpha = jnp.exp2(m - m_new)
        p = jnp.exp2(s - m_new)
        m_ref[...] = m_new
        l_ref[...] = alpha * l_ref[...] + jnp.sum(p, axis=-1, keepdims=True)
        acc_ref[...] = alpha * acc_ref[...] + jnp.dot(p.astype(BF16), v,
                                                      preferred_element_type=F32)

    def finish():
        o_ref[...] = (acc_ref[...] / l_ref[...]).astype(o_ref.dtype)

    m_ref[...] = jnp.full(m_ref.shape, -jnp.inf, F32)
    l_ref[...] = jnp.zeros(l_ref.shape, F32)
    acc_ref[...] = jnp.zeros(acc_ref.shape, F32)
    sa_ref[...] = scores(0)

    @pl.loop(0, qi // 2)
    def _(jj):
        sb_ref[...] = scores(2 * jj + 1)
        update(2 * jj, sa_ref)
        sa_ref[...] = scores(2 * jj + 2)
        update(2 * jj + 1, sb_ref)

    @pl.when(qi % 2 == 1)
    def _():
        sb_ref[...] = scores(qi)
        update(qi - 1, sa_ref)
        update(qi, sb_ref, diagonal=True)
        finish()

    @pl.when(qi % 2 == 0)
    def _():
        update(qi, sa_ref, diagonal=True)
        finish()


def _attention(q, k, v, tile=1024):
    batch, heads, seq, _ = q.shape
    return pl.pallas_call(
        functools.partial(_attn_kernel, tile=tile),
        grid=(batch, heads, seq // tile),
        in_specs=[pl.BlockSpec((None, None, tile, QK_DIM), lambda b, h, i: (b, h, i, 0)),
                  pl.BlockSpec((None, None, seq, QK_DIM), lambda b, h, i: (b, h, 0, 0)),
                  pl.BlockSpec((None, None, seq, V_HEAD_DIM), lambda b, h, i: (b, h, 0, 0))],
        out_specs=pl.BlockSpec((None, tile, V_HEAD_DIM), lambda b, h, i: (b, i, h)),
        out_shape=jax.ShapeDtypeStruct((batch, seq, heads * V_HEAD_DIM), BF16),
        scratch_shapes=[pltpu.VMEM((tile, tile), F32), pltpu.VMEM((tile, tile), F32),
                        pltpu.VMEM((tile, 1), F32), pltpu.VMEM((tile, 1), F32),
                        pltpu.VMEM((tile, V_HEAD_DIM), F32)],
        compiler_params=_params(3),
        name="attention",
    )(q, k, v)


def _mla_out_kernel(x_ref, a_ref, z_ref, w_ref, gate_ref, o_ref):
    y = (a_ref[...].astype(F32) * z_ref[...].astype(F32)).astype(BF16)
    out = jnp.dot(y, w_ref[...], preferred_element_type=F32)
    o_ref[...] = x_ref[...] + gate_ref[...] * out


def _mla_out(x, a, z, w_out, gate, seq, tm=512):
    t, d = x.shape
    per_batch = seq // tm
    row = lambda w: pl.BlockSpec((tm, w), lambda i: (i, 0))
    return pl.pallas_call(
        _mla_out_kernel,
        grid=(t // tm,),
        in_specs=[row(d), row(ATTN_WIDTH), row(ATTN_WIDTH), _resident(w_out.shape),
                  pl.BlockSpec((None, 1, d), lambda i: (i // per_batch, 0, 0))],
        out_specs=row(d),
        out_shape=jax.ShapeDtypeStruct((t, d), F32),
        compiler_params=_params(1),
        name="mla_out",
    )(x, a, z, w_out, gate)


def _pad_head_columns(w, head_dim):
    k = w.shape[0]
    w = w.reshape(k, N_HEADS, head_dim)
    w = jnp.pad(w, ((0, 0), (0, 0), (0, HEAD_SLAB - head_dim)))
    return w.reshape(k, N_HEADS * HEAD_SLAB)


def _rope_tables(seq):
    pos = jnp.arange(seq, dtype=F32)
    inv_freq = ROPE_THETA ** (-jnp.arange(0, QK_ROPE_DIM, 2, dtype=F32) / QK_ROPE_DIM)
    ang = pos[:, None] * inv_freq[None, :]
    cos, sin = jnp.cos(ang), jnp.sin(ang)
    zero = jnp.zeros_like(cos)
    pad = jnp.zeros((seq, LANES - QK_ROPE_DIM), F32)
    return (jnp.concatenate([cos, cos, pad], axis=1),
            jnp.concatenate([zero, sin, pad], axis=1),
            jnp.concatenate([-sin, zero, pad], axis=1))


def _pad_lanes(g):
    return jnp.pad(g, (0, LANES - g.shape[0])).reshape(1, LANES)


def kernel(x, c, ada_w, ada_b, norm_g, a_w_in, a_ln_g, a_ln_b, a_w_s, a_b_s, a_w_out, kv_ada_w, kv_ada_b, kv_norm_g, kv_w_dkv, kv_g_kva, kv_w_ukv, kv_g_kn, kv_g_kr, b_w_in, b_g_qa, b_w_uq, b_g_qn, b_g_qr, b_w_out):
    batch, seq, d = x.shape
    t = batch * seq
    sm_scale = QK_DIM ** -0.5 * math.log2(math.e)

    c_pad = jnp.pad(c, ((0, 8 - batch), (0, 0)))
    mod = _ada_mod(c_pad, ada_w, ada_b)[:, :batch]
    kv_mod = _ada_mod(c_pad, kv_ada_w[None], kv_ada_b[None])[0, :batch]
    vecs = lambda m, n: [v.reshape(batch, 1, d) for v in jnp.split(m, n, axis=-1)]

    cos, s_hi, s_lo = _rope_tables(seq)
    xf = x.reshape(t, d)

    for i in range(N_A_LAYERS):
        shift, scale, gate = vecs(mod[i], 3)
        uvz = _gmlp_in(xf, norm_g[i].reshape(1, d), scale, shift, a_w_in[i].astype(BF16), seq)
        xf = _gmlp_out(xf, uvz, a_ln_g[i].reshape(1, GM_WIDTH), a_ln_b[i].reshape(1, GM_WIDTH),
                       a_w_s[i], a_b_s[i].T, a_w_out[i].astype(BF16), gate, seq)

    kv_shift, kv_scale = vecs(kv_mod, 2)
    wd = jnp.pad(kv_w_dkv, ((0, 0), (0, LANES - QK_ROPE_DIM))).astype(BF16)
    k_all, v_all = _kv_proj(xf, kv_norm_g.reshape(1, d), kv_scale, kv_shift, wd,
                            kv_g_kva.reshape(1, KV_LORA_RANK), _pad_lanes(kv_g_kr),
                            kv_w_ukv.astype(BF16), kv_g_kn.reshape(1, QK_NOPE_DIM),
                            cos, s_hi, s_lo, batch, seq)

    for j in range(N_B_LAYERS):
        shift, scale, gate = vecs(mod[N_A_LAYERS + j], 3)
        q, z = _mla_in(xf, norm_g[N_A_LAYERS + j].reshape(1, d), scale, shift,
                       b_w_in[j].astype(BF16), b_g_qa[j].reshape(1, Q_LORA_RANK),
                       _pad_head_columns(b_w_uq[j], QK_DIM).astype(BF16),
                       (b_g_qn[j] * sm_scale).reshape(1, QK_NOPE_DIM),
                       _pad_lanes(b_g_qr[j] * sm_scale), cos, s_hi, s_lo, batch, seq)
        a = _attention(q, k_all, v_all)
        xf = _mla_out(xf, a.reshape(t, ATTN_WIDTH), z, b_w_out[j].astype(BF16), gate, seq)

    return xf.reshape(batch, seq, d)
```

```python
import functools
import math

import jax
import jax.numpy as jnp
from jax import lax
from jax.experimental import pallas as pl
from jax.experimental.pallas import tpu as pltpu

F32 = jnp.float32
BF16 = jnp.bfloat16

D_MODEL = 2048
N_A_LAYERS = 2
N_B_LAYERS = 2
DEPTH = N_A_LAYERS + N_B_LAYERS
GM_WIDTH = 2 * D_MODEL
GM_GROUPS = 16
GM_GROUP_DIM = GM_WIDTH // GM_GROUPS
CHUNK = 128
N_HEADS = 16
QK_NOPE_DIM = 128
QK_ROPE_DIM = 64
HALF_ROPE = QK_ROPE_DIM // 2
QK_DIM = QK_NOPE_DIM + QK_ROPE_DIM
V_HEAD_DIM = 128
Q_LORA_RANK = 512
KV_LORA_RANK = 512
ATTN_WIDTH = N_HEADS * V_HEAD_DIM
ROPE_THETA = 10000.0
EPS = 1e-6

LANES = 128
HEAD_SLAB = 2 * LANES
VMEM_LIMIT = 56 * 1024 * 1024

GELU_C = math.sqrt(2.0 / math.pi)


def _silu(a):
    return a / (1.0 + jnp.exp(-a))


def _rms(a, width):
    return a * lax.rsqrt(jnp.sum(a * a, axis=-1, keepdims=True) * (1.0 / width) + EPS)


def _ada_norm(x, g, scale, shift):
    return (_rms(x, x.shape[-1]) * (g * (1.0 + scale)) + shift).astype(BF16)


def _rope(r, cos, sin_hi, sin_lo):
    return (r * cos + pltpu.roll(r, HALF_ROPE, 1) * sin_hi
            + pltpu.roll(r, LANES - HALF_ROPE, 1) * sin_lo)


def _params(n_axes):
    return pltpu.CompilerParams(dimension_semantics=("arbitrary",) * n_axes,
                                vmem_limit_bytes=VMEM_LIMIT)


def _resident(shape):
    return pl.BlockSpec(shape, lambda *_: (0,) * len(shape), pipeline_mode=pl.Buffered(1))


def _ada_kernel(c_ref, w_ref, b_ref, o_ref):
    a = _silu(c_ref[...]).astype(BF16)
    o_ref[...] = jnp.dot(a, w_ref[...].astype(BF16), preferred_element_type=F32) + b_ref[...]


def _ada_mod(c_pad, w, b, tn=1024):
    n_layers, d, n = w.shape
    return pl.pallas_call(
        _ada_kernel,
        grid=(n_layers, n // tn),
        in_specs=[pl.BlockSpec(c_pad.shape, lambda l, j: (0, 0)),
                  pl.BlockSpec((None, d, tn), lambda l, j: (l, 0, j)),
                  pl.BlockSpec((None, 1, tn), lambda l, j: (l, 0, j))],
        out_specs=pl.BlockSpec((None, c_pad.shape[0], tn), lambda l, j: (l, 0, j)),
        out_shape=jax.ShapeDtypeStruct((n_layers, c_pad.shape[0], n), F32),
        compiler_params=_params(2),
        name="ada_mod",
    )(c_pad, w, b.reshape(n_layers, 1, n))


def _gmlp_in_kernel(x_ref, g_ref, sc_ref, sh_ref, w_ref, o_ref, h_ref, *, n_gelu_tiles, chunk):
    j = pl.program_id(1)

    @pl.when(j == 0)
    def _():
        h_ref[...] = _ada_norm(x_ref[...], g_ref[...], sc_ref[...], sh_ref[...])

    is_gelu = j < n_gelu_tiles
    c1 = jnp.where(is_gelu, 2.0 * GELU_C, 1.0).astype(F32)
    c3 = jnp.where(is_gelu, 2.0 * GELU_C * 0.044715, 0.0).astype(F32)
    h = h_ref[...]
    for n0 in range(0, o_ref.shape[1], chunk):
        a = jnp.dot(h, w_ref[:, n0:n0 + chunk], preferred_element_type=F32)
        gate = a * (c1 + c3 * (a * a))
        o_ref[:, n0:n0 + chunk] = (a / (1.0 + jnp.exp(-gate))).astype(BF16)


def _gmlp_in(x, g, scale, shift, w, seq, tm=512, tn=2048, chunk=512):
    t, d = x.shape
    n = w.shape[1]
    per_batch = seq // tm
    vec = pl.BlockSpec((None, 1, d), lambda i, j: (i // per_batch, 0, 0))
    return pl.pallas_call(
        functools.partial(_gmlp_in_kernel, n_gelu_tiles=2 * GM_WIDTH // tn, chunk=chunk),
        grid=(t // tm, n // tn),
        in_specs=[pl.BlockSpec((tm, d), lambda i, j: (i, 0)),
                  pl.BlockSpec((1, d), lambda i, j: (0, 0)),
                  vec, vec,
                  pl.BlockSpec((d, tn), lambda i, j: (0, j))],
        out_specs=pl.BlockSpec((tm, tn), lambda i, j: (i, j)),
        out_shape=jax.ShapeDtypeStruct((t, n), BF16),
        scratch_shapes=[pltpu.VMEM((tm, d), BF16)],
        compiler_params=_params(2),
        name="gmlp_in",
    )(x, g, scale, shift, w)


def _gmlp_out_kernel(x_ref, u_ref, v_ref, z_ref, lng_ref, lnb_ref, ws_ref, bst_ref, w_ref,
                     gate_ref, o_ref, wsm_ref, vn_ref, y_ref):
    @pl.when(pl.program_id(0) == 0)
    def _():
        row = lax.broadcasted_iota(jnp.int32, (CHUNK, CHUNK), 0)
        col = lax.broadcasted_iota(jnp.int32, (CHUNK, CHUNK), 1)
        for g in range(GM_GROUPS):
            wsm_ref[g] = jnp.where(col <= row, ws_ref[g], 0.0).astype(BF16)

    v = v_ref[...].astype(F32)
    mu = jnp.mean(v, axis=-1, keepdims=True)
    vc = v - mu
    var = jnp.mean(vc * vc, axis=-1, keepdims=True)
    vn_ref[...] = (vc * lax.rsqrt(var + EPS) * lng_ref[...] + lnb_ref[...]).astype(BF16)

    tm = x_ref.shape[0]
    for c in range(tm // CHUNK):
        rows = pl.ds(c * CHUNK, CHUNK)
        for g in range(GM_GROUPS):
            cols = pl.ds(g * GM_GROUP_DIM, GM_GROUP_DIM)
            mixed = jnp.dot(wsm_ref[g], vn_ref[rows, cols], preferred_element_type=F32)
            mixed = mixed + bst_ref[:, g:g + 1]
            y = u_ref[rows, cols].astype(F32) * mixed * z_ref[rows, cols].astype(F32)
            y_ref[rows, cols] = y.astype(BF16)

    out = jnp.dot(y_ref[...], w_ref[...], preferred_element_type=F32)
    o_ref[...] = x_ref[...] + gate_ref[...] * out


def _gmlp_out(x, uvz, ln_g, ln_b, w_s, b_s_t, w_out, gate, seq, tm=256):
    t, d = x.shape
    per_batch = seq // tm
    part = lambda k: pl.BlockSpec((tm, GM_WIDTH), lambda i: (i, k))
    return pl.pallas_call(
        _gmlp_out_kernel,
        grid=(t // tm,),
        in_specs=[pl.BlockSpec((tm, d), lambda i: (i, 0)),
                  part(0), part(1), part(2),
                  _resident((1, GM_WIDTH)), _resident((1, GM_WIDTH)),
                  _resident(w_s.shape), _resident(b_s_t.shape), _resident(w_out.shape),
                  pl.BlockSpec((None, 1, d), lambda i: (i // per_batch, 0, 0))],
        out_specs=pl.BlockSpec((tm, d), lambda i: (i, 0)),
        out_shape=jax.ShapeDtypeStruct((t, d), F32),
        scratch_shapes=[pltpu.VMEM((GM_GROUPS, CHUNK, CHUNK), BF16),
                        pltpu.VMEM((tm, GM_WIDTH), BF16),
                        pltpu.VMEM((tm, GM_WIDTH), BF16)],
        compiler_params=_params(1),
        name="gmlp_out",
    )(x, uvz, uvz, uvz, ln_g, ln_b, w_s, b_s_t, w_out, gate)


def _kv_kernel(x_ref, g_ref, sc_ref, sh_ref, wd_ref, gkva_ref, gkr_ref, wu_ref, gkn_ref,
               cos_ref, shi_ref, slo_ref, sgn_ref, k_ref, vt_ref, *, heads_per_chunk):
    h = _ada_norm(x_ref[...], g_ref[...], sc_ref[...], sh_ref[...])
    ckr = jnp.dot(h, wd_ref[...], preferred_element_type=F32)
    c_kv = (_rms(ckr[:, :KV_LORA_RANK], KV_LORA_RANK) * gkva_ref[...]).astype(BF16)
    k_r = _rms(ckr[:, KV_LORA_RANK:], QK_ROPE_DIM) * gkr_ref[...]
    k_r = _rope(k_r, cos_ref[...], shi_ref[...], slo_ref[...])
    k_x = (k_r + pltpu.roll(k_r, QK_ROPE_DIM, 1) * sgn_ref[...]).astype(BF16)
    for c0 in range(0, N_HEADS, heads_per_chunk):
        kv = jnp.dot(c_kv, wu_ref[:, c0 * HEAD_SLAB:(c0 + heads_per_chunk) * HEAD_SLAB],
                     preferred_element_type=F32)
        for i in range(heads_per_chunk):
            base = i * HEAD_SLAB
            k_n = _rms(kv[:, base:base + QK_NOPE_DIM], QK_NOPE_DIM) * gkn_ref[...]
            k_ref[c0 + i, :, :QK_NOPE_DIM] = k_n.astype(BF16)
            k_ref[c0 + i, :, QK_NOPE_DIM:] = k_x
            vt_ref[c0 + i] = kv[:, base + QK_NOPE_DIM:base + HEAD_SLAB].T.astype(BF16)


def _kv_proj(x, g, scale, shift, wd, gkva, gkr, wu, gkn, cos, s_hi, s_lo, sgn, batch, seq,
             tm=512, heads_per_chunk=2):
    t, d = x.shape
    per_batch = seq // tm
    vec = pl.BlockSpec((None, 1, d), lambda i: (i // per_batch, 0, 0))
    tab = pl.BlockSpec((tm, LANES), lambda i: (i % per_batch, 0))
    return pl.pallas_call(
        functools.partial(_kv_kernel, heads_per_chunk=heads_per_chunk),
        grid=(t // tm,),
        in_specs=[pl.BlockSpec((tm, d), lambda i: (i, 0)),
                  _resident((1, d)), vec, vec,
                  _resident(wd.shape), _resident(gkva.shape), _resident(gkr.shape),
                  _resident(wu.shape), _resident(gkn.shape), tab, tab, tab,
                  _resident(sgn.shape)],
        out_specs=[pl.BlockSpec((None, N_HEADS, tm, HEAD_SLAB),
                                lambda i: (i // per_batch, 0, i % per_batch, 0)),
                   pl.BlockSpec((None, N_HEADS, V_HEAD_DIM, tm),
                                lambda i: (i // per_batch, 0, 0, i % per_batch))],
        out_shape=[jax.ShapeDtypeStruct((batch, N_HEADS, seq, HEAD_SLAB), BF16),
                   jax.ShapeDtypeStruct((batch, N_HEADS, V_HEAD_DIM, seq), BF16)],
        compiler_params=_params(1),
        name="kv_proj",
    )(x, g, scale, shift, wd, gkva, gkr, wu, gkn, cos, s_hi, s_lo, sgn)


def _mla_in_kernel(x_ref, g_ref, sc_ref, sh_ref, win_ref, gqa_ref, wuq_ref, gqn_ref, gqr_ref,
                   tab_ref, q_ref, z_ref, *, chunk, heads_per_chunk):
    h = _ada_norm(x_ref[...], g_ref[...], sc_ref[...], sh_ref[...])
    c_q = jnp.dot(h, win_ref[:, :Q_LORA_RANK], preferred_element_type=F32)
    c_q = (_rms(c_q, Q_LORA_RANK) * gqa_ref[...]).astype(BF16)
    for n0 in range(0, ATTN_WIDTH, chunk):
        z = jnp.dot(h, win_ref[:, Q_LORA_RANK + n0:Q_LORA_RANK + n0 + chunk],
                    preferred_element_type=F32)
        z_ref[:, n0:n0 + chunk] = _silu(z).astype(BF16)
    rope_mult = tab_ref[...] * gqr_ref[...]
    for c0 in range(0, N_HEADS, heads_per_chunk):
        q = jnp.dot(c_q, wuq_ref[:, c0 * HEAD_SLAB:(c0 + heads_per_chunk) * HEAD_SLAB],
                    preferred_element_type=F32)
        for i in range(heads_per_chunk):
            q_n = q[:, i * HEAD_SLAB:i * HEAD_SLAB + QK_NOPE_DIM]
            q_x = q[:, i * HEAD_SLAB + QK_NOPE_DIM:(i + 1) * HEAD_SLAB]
            q_ref[c0 + i, :, :QK_NOPE_DIM] = (_rms(q_n, LANES) * gqn_ref[...]).astype(BF16)
            q_ref[c0 + i, :, QK_NOPE_DIM:] = (_rms(q_x, LANES) * rope_mult).astype(BF16)


def _mla_in(x, g, scale, shift, w_in, gqa, wuq, gqn, gqr, tab, batch, seq, tm=512, chunk=512,
            heads_per_chunk=2):
    t, d = x.shape
    per_batch = seq // tm
    vec = pl.BlockSpec((None, 1, d), lambda i: (i // per_batch, 0, 0))
    return pl.pallas_call(
        functools.partial(_mla_in_kernel, chunk=chunk, heads_per_chunk=heads_per_chunk),
        grid=(t // tm,),
        in_specs=[pl.BlockSpec((tm, d), lambda i: (i, 0)),
                  _resident((1, d)), vec, vec,
                  _resident(w_in.shape), _resident(gqa.shape), _resident(wuq.shape),
                  _resident(gqn.shape), _resident(gqr.shape),
                  pl.BlockSpec((tm, LANES), lambda i: (i % per_batch, 0))],
        out_specs=[pl.BlockSpec((None, N_HEADS, tm, HEAD_SLAB),
                                lambda i: (i // per_batch, 0, i % per_batch, 0)),
                   pl.BlockSpec((tm, ATTN_WIDTH), lambda i: (i, 0))],
        out_shape=[jax.ShapeDtypeStruct((batch, N_HEADS, seq, HEAD_SLAB), BF16),
                   jax.ShapeDtypeStruct((t, ATTN_WIDTH), BF16)],
        compiler_params=_params(1),
        name="mla_in",
    )(x, g, scale, shift, w_in, gqa, wuq, gqn, gqr, tab)


def _attn_kernel(q_ref, k_ref, vt_ref, o_ref, qt_ref, sa_ref, sb_ref, m_ref, l_ref, acc_ref, *,
                 tile):
    qi = pl.program_id(2)
    qt_ref[...] = q_ref[...].astype(F32).T.astype(BF16)

    def scores(j):
        k = k_ref[pl.ds(pl.multiple_of(j * tile, tile), tile), :]
        return jnp.dot(k, qt_ref[...], preferred_element_type=F32)

    def update(j, s_ref, diagonal=False):
        vt = vt_ref[:, pl.ds(pl.multiple_of(j * tile, tile), tile)]
        s = s_ref[...]
        if diagonal:
            key = lax.broadcasted_iota(jnp.int32, s.shape, 0)
            query = lax.broadcasted_iota(jnp.int32, s.shape, 1)
            s = jnp.where(key <= query, s, -jnp.inf)
        m = m_ref[...]
        m_new = jnp.maximum(m, jnp.max(s, axis=0, keepdims=True))
        alpha = jnp.exp2(m - m_new)
        p = jnp.exp2(s - m_new)
        m_ref[...] = m_new
        l_ref[...] = alpha * l_ref[...] + jnp.sum(p, axis=0, keepdims=True)
        acc_ref[...] = alpha * acc_ref[...] + jnp.dot(vt, p.astype(BF16),
                                                      preferred_element_type=F32)

    def finish():
        o_ref[...] = (acc_ref[...] / l_ref[...]).T.astype(o_ref.dtype)

    m_ref[...] = jnp.full(m_ref.shape, -jnp.inf, F32)
    l_ref[...] = jnp.zeros(l_ref.shape, F32)
    acc_ref[...] = jnp.zeros(acc_ref.shape, F32)
    sa_ref[...] = scores(0)

    @pl.loop(0, qi // 2)
    def _(jj):
        sb_ref[...] = scores(2 * jj + 1)
        update(2 * jj, sa_ref)
        sa_ref[...] = scores(2 * jj + 2)
        update(2 * jj + 1, sb_ref)

    @pl.when(qi % 2 == 1)
    def _():
        sb_ref[...] = scores(qi)
        update(qi - 1, sa_ref)
        update(qi, sb_ref, diagonal=True)
        finish()

    @pl.when(qi % 2 == 0)
    def _():
        update(qi, sa_ref, diagonal=True)
        finish()


def _attention(q, k, vt, tile=1024):
    batch, heads, seq, width = q.shape
    return pl.pallas_call(
        functools.partial(_attn_kernel, tile=tile),
        grid=(batch, heads, seq // tile),
        in_specs=[pl.BlockSpec((None, None, tile, width), lambda b, h, i: (b, h, i, 0)),
                  pl.BlockSpec((None, None, seq, width), lambda b, h, i: (b, h, 0, 0)),
                  pl.BlockSpec((None, None, V_HEAD_DIM, seq), lambda b, h, i: (b, h, 0, 0))],
        out_specs=pl.BlockSpec((None, tile, V_HEAD_DIM), lambda b, h, i: (b, i, h)),
        out_shape=jax.ShapeDtypeStruct((batch, seq, heads * V_HEAD_DIM), BF16),
        scratch_shapes=[pltpu.VMEM((width, tile), BF16),
                        pltpu.VMEM((tile, tile), F32), pltpu.VMEM((tile, tile), F32),
                        pltpu.VMEM((1, tile), F32), pltpu.VMEM((1, tile), F32),
                        pltpu.VMEM((V_HEAD_DIM, tile), F32)],
        compiler_params=_params(3),
        name="attention",
    )(q, k, vt)


def _mla_out_kernel(x_ref, a_ref, z_ref, w_ref, gate_ref, o_ref):
    y = (a_ref[...].astype(F32) * z_ref[...].astype(F32)).astype(BF16)
    out = jnp.dot(y, w_ref[...], preferred_element_type=F32)
    o_ref[...] = x_ref[...] + gate_ref[...] * out


def _mla_out(x, a, z, w_out, gate, seq, tm=512):
    t, d = x.shape
    per_batch = seq // tm
    row = lambda w: pl.BlockSpec((tm, w), lambda i: (i, 0))
    return pl.pallas_call(
        _mla_out_kernel,
        grid=(t // tm,),
        in_specs=[row(d), row(ATTN_WIDTH), row(ATTN_WIDTH), _resident(w_out.shape),
                  pl.BlockSpec((None, 1, d), lambda i: (i // per_batch, 0, 0))],
        out_specs=row(d),
        out_shape=jax.ShapeDtypeStruct((t, d), F32),
        compiler_params=_params(1),
        name="mla_out",
    )(x, a, z, w_out, gate)


def _rope_halves(r):
    return r[..., :HALF_ROPE], r[..., HALF_ROPE:]


def _query_slabs(w):
    k = w.shape[0]
    w = w.reshape(k, N_HEADS, QK_DIM)
    x1, x2 = _rope_halves(w[..., QK_NOPE_DIM:])
    w = jnp.concatenate([w[..., :QK_NOPE_DIM], x1, x2, x2, x1], axis=-1)
    return w.reshape(k, N_HEADS * HEAD_SLAB)


def _rope_tables(seq):
    pos = jnp.arange(seq, dtype=F32)
    inv_freq = ROPE_THETA ** (-jnp.arange(0, QK_ROPE_DIM, 2, dtype=F32) / QK_ROPE_DIM)
    ang = pos[:, None] * inv_freq[None, :]
    cos, sin = jnp.cos(ang), jnp.sin(ang)
    zero = jnp.zeros_like(cos)
    pad = jnp.zeros((seq, LANES - QK_ROPE_DIM), F32)
    key_tables = (jnp.concatenate([cos, cos, pad], axis=1),
                  jnp.concatenate([zero, sin, pad], axis=1),
                  jnp.concatenate([-sin, zero, pad], axis=1))
    query_table = jnp.concatenate([cos, cos, sin, sin], axis=1)
    return key_tables, query_table


def _pad_lanes(g):
    return jnp.pad(g, (0, LANES - g.shape[0])).reshape(1, LANES)


def kernel(x, c, ada_w, ada_b, norm_g, a_w_in, a_ln_g, a_ln_b, a_w_s, a_b_s, a_w_out, kv_ada_w, kv_ada_b, kv_norm_g, kv_w_dkv, kv_g_kva, kv_w_ukv, kv_g_kn, kv_g_kr, b_w_in, b_g_qa, b_w_uq, b_g_qn, b_g_qr, b_w_out):
    batch, seq, d = x.shape
    t = batch * seq
    sm_scale = QK_DIM ** -0.5 * math.log2(math.e)

    c_pad = jnp.pad(c, ((0, 8 - batch), (0, 0)))
    mod = _ada_mod(c_pad, ada_w, ada_b)[:, :batch]
    kv_mod = _ada_mod(c_pad, kv_ada_w[None], kv_ada_b[None])[0, :batch]
    vecs = lambda m, n: [v.reshape(batch, 1, d) for v in jnp.split(m, n, axis=-1)]

    (cos, s_hi, s_lo), q_tab = _rope_tables(seq)
    sgn = jnp.concatenate([jnp.zeros((QK_ROPE_DIM,), F32), -jnp.ones((HALF_ROPE,), F32),
                           jnp.ones((HALF_ROPE,), F32)]).reshape(1, LANES)
    xf = x.reshape(t, d)

    for i in range(N_A_LAYERS):
        shift, scale, gate = vecs(mod[i], 3)
        uvz = _gmlp_in(xf, norm_g[i].reshape(1, d), scale, shift, a_w_in[i].astype(BF16), seq)
        xf = _gmlp_out(xf, uvz, a_ln_g[i].reshape(1, GM_WIDTH), a_ln_b[i].reshape(1, GM_WIDTH),
                       a_w_s[i], a_b_s[i].T, a_w_out[i].astype(BF16), gate, seq)

    kv_shift, kv_scale = vecs(kv_mod, 2)
    wd = jnp.pad(kv_w_dkv, ((0, 0), (0, LANES - QK_ROPE_DIM))).astype(BF16)
    k_all, vt_all = _kv_proj(xf, kv_norm_g.reshape(1, d), kv_scale, kv_shift, wd,
                             kv_g_kva.reshape(1, KV_LORA_RANK), _pad_lanes(kv_g_kr),
                             kv_w_ukv.astype(BF16), kv_g_kn.reshape(1, QK_NOPE_DIM),
                             cos, s_hi, s_lo, sgn, batch, seq)

    for j in range(N_B_LAYERS):
        shift, scale, gate = vecs(mod[N_A_LAYERS + j], 3)
        g1, g2 = _rope_halves(b_g_qr[j] * sm_scale)
        q, z = _mla_in(xf, norm_g[N_A_LAYERS + j].reshape(1, d), scale, shift,
                       b_w_in[j].astype(BF16), b_g_qa[j].reshape(1, Q_LORA_RANK),
                       _query_slabs(b_w_uq[j]).astype(BF16),
                       (b_g_qn[j] * sm_scale).reshape(1, QK_NOPE_DIM),
                       jnp.concatenate([g1, g2, g2, g1]).reshape(1, LANES), q_tab, batch, seq)
        a = _attention(q, k_all, vt_all)
        xf = _mla_out(xf, a.reshape(t, ATTN_WIDTH), z, b_w_out[j].astype(BF16), gate, seq)

    return xf.reshape(batch, seq, d)
```

```python
import functools
import math

import jax
import jax.numpy as jnp
from jax import lax
from jax.experimental import pallas as pl
from jax.experimental.pallas import tpu as pltpu

F32 = jnp.float32
BF16 = jnp.bfloat16

D_MODEL = 2048
N_A_LAYERS = 2
N_B_LAYERS = 2
DEPTH = N_A_LAYERS + N_B_LAYERS
GM_WIDTH = 2 * D_MODEL
GM_GROUPS = 16
GM_GROUP_DIM = GM_WIDTH // GM_GROUPS
CHUNK = 128
N_HEADS = 16
QK_NOPE_DIM = 128
QK_ROPE_DIM = 64
HALF_ROPE = QK_ROPE_DIM // 2
QK_DIM = QK_NOPE_DIM + QK_ROPE_DIM
V_HEAD_DIM = 128
Q_LORA_RANK = 512
KV_LORA_RANK = 512
ATTN_WIDTH = N_HEADS * V_HEAD_DIM
ROPE_THETA = 10000.0
EPS = 1e-6

LANES = 128
HEAD_SLAB = 2 * LANES
VMEM_LIMIT = 56 * 1024 * 1024

GELU_C = math.sqrt(2.0 / math.pi)


def _silu(a):
    return a / (1.0 + jnp.exp(-a))


def _rms(a, width):
    return a * lax.rsqrt(jnp.sum(a * a, axis=-1, keepdims=True) * (1.0 / width) + EPS)


def _ada_norm(x, g, scale, shift):
    return (_rms(x, x.shape[-1]) * (g * (1.0 + scale)) + shift).astype(BF16)


def _rope(r, cos, sin_hi, sin_lo):
    return (r * cos + pltpu.roll(r, HALF_ROPE, 1) * sin_hi
            + pltpu.roll(r, LANES - HALF_ROPE, 1) * sin_lo)


def _params(n_axes):
    return pltpu.CompilerParams(dimension_semantics=("arbitrary",) * n_axes,
                                vmem_limit_bytes=VMEM_LIMIT)


def _resident(shape):
    return pl.BlockSpec(shape, lambda *_: (0,) * len(shape), pipeline_mode=pl.Buffered(1))


def _ada_kernel(c_ref, w_ref, b_ref, o_ref):
    a = _silu(c_ref[...]).astype(BF16)
    o_ref[...] = jnp.dot(a, w_ref[...].astype(BF16), preferred_element_type=F32) + b_ref[...]


def _ada_mod(c_pad, w, b, tn=1024):
    n_layers, d, n = w.shape
    return pl.pallas_call(
        _ada_kernel,
        grid=(n_layers, n // tn),
        in_specs=[pl.BlockSpec(c_pad.shape, lambda l, j: (0, 0)),
                  pl.BlockSpec((None, d, tn), lambda l, j: (l, 0, j)),
                  pl.BlockSpec((None, 1, tn), lambda l, j: (l, 0, j))],
        out_specs=pl.BlockSpec((None, c_pad.shape[0], tn), lambda l, j: (l, 0, j)),
        out_shape=jax.ShapeDtypeStruct((n_layers, c_pad.shape[0], n), F32),
        compiler_params=_params(2),
        name="ada_mod",
    )(c_pad, w, b.reshape(n_layers, 1, n))


def _gmlp_in_kernel(x_ref, g_ref, sc_ref, sh_ref, w_ref, o_ref, h_ref, *, n_gelu_tiles, chunk):
    j = pl.program_id(1)

    @pl.when(j == 0)
    def _():
        h_ref[...] = _ada_norm(x_ref[...], g_ref[...], sc_ref[...], sh_ref[...])

    is_gelu = j < n_gelu_tiles
    c1 = jnp.where(is_gelu, 2.0 * GELU_C, 1.0).astype(F32)
    c3 = jnp.where(is_gelu, 2.0 * GELU_C * 0.044715, 0.0).astype(F32)
    h = h_ref[...]
    for n0 in range(0, o_ref.shape[1], chunk):
        a = jnp.dot(h, w_ref[:, n0:n0 + chunk], preferred_element_type=F32)
        gate = a * (c1 + c3 * (a * a))
        o_ref[:, n0:n0 + chunk] = (a / (1.0 + jnp.exp(-gate))).astype(BF16)


def _gmlp_in(x, g, scale, shift, w, seq, tm=1024, tn=2048, chunk=512):
    t, d = x.shape
    n = w.shape[1]
    per_batch = seq // tm
    vec = pl.BlockSpec((None, 1, d), lambda i, j: (i // per_batch, 0, 0))
    return pl.pallas_call(
        functools.partial(_gmlp_in_kernel, n_gelu_tiles=2 * GM_WIDTH // tn, chunk=chunk),
        grid=(t // tm, n // tn),
        in_specs=[pl.BlockSpec((tm, d), lambda i, j: (i, 0)),
                  pl.BlockSpec((1, d), lambda i, j: (0, 0)),
                  vec, vec,
                  pl.BlockSpec((d, tn), lambda i, j: (0, j))],
        out_specs=pl.BlockSpec((tm, tn), lambda i, j: (i, j)),
        out_shape=jax.ShapeDtypeStruct((t, n), BF16),
        scratch_shapes=[pltpu.VMEM((tm, d), BF16)],
        compiler_params=_params(2),
        name="gmlp_in",
    )(x, g, scale, shift, w)


def _gmlp_out_kernel(x_ref, u_ref, v_ref, z_ref, lng_ref, lnb_ref, ws_ref, bst_ref, w_ref,
                     gate_ref, o_ref, wsm_ref, vn_ref, y_ref):
    @pl.when(pl.program_id(0) == 0)
    def _():
        row = lax.broadcasted_iota(jnp.int32, (CHUNK, CHUNK), 0)
        col = lax.broadcasted_iota(jnp.int32, (CHUNK, CHUNK), 1)
        for g in range(GM_GROUPS):
            wsm_ref[g] = jnp.where(col <= row, ws_ref[g], 0.0).astype(BF16)

    v = v_ref[...].astype(F32)
    mu = jnp.mean(v, axis=-1, keepdims=True)
    vc = v - mu
    var = jnp.mean(vc * vc, axis=-1, keepdims=True)
    vn_ref[...] = (vc * lax.rsqrt(var + EPS) * lng_ref[...] + lnb_ref[...]).astype(BF16)

    tm = x_ref.shape[0]
    for c in range(tm // CHUNK):
        rows = pl.ds(c * CHUNK, CHUNK)
        for g in range(GM_GROUPS):
            cols = pl.ds(g * GM_GROUP_DIM, GM_GROUP_DIM)
            mixed = jnp.dot(wsm_ref[g], vn_ref[rows, cols], preferred_element_type=F32)
            mixed = mixed + bst_ref[:, g:g + 1]
            y = u_ref[rows, cols].astype(F32) * mixed * z_ref[rows, cols].astype(F32)
            y_ref[rows, cols] = y.astype(BF16)

    out = jnp.dot(y_ref[...], w_ref[...], preferred_element_type=F32)
    o_ref[...] = x_ref[...] + gate_ref[...] * out


def _gmlp_out(x, uvz, ln_g, ln_b, w_s, b_s_t, w_out, gate, seq, tm=256):
    t, d = x.shape
    per_batch = seq // tm
    part = lambda k: pl.BlockSpec((tm, GM_WIDTH), lambda i: (i, k))
    return pl.pallas_call(
        _gmlp_out_kernel,
        grid=(t // tm,),
        in_specs=[pl.BlockSpec((tm, d), lambda i: (i, 0)),
                  part(0), part(1), part(2),
                  _resident((1, GM_WIDTH)), _resident((1, GM_WIDTH)),
                  _resident(w_s.shape), _resident(b_s_t.shape), _resident(w_out.shape),
                  pl.BlockSpec((None, 1, d), lambda i: (i // per_batch, 0, 0))],
        out_specs=pl.BlockSpec((tm, d), lambda i: (i, 0)),
        out_shape=jax.ShapeDtypeStruct((t, d), F32),
        scratch_shapes=[pltpu.VMEM((GM_GROUPS, CHUNK, CHUNK), BF16),
                        pltpu.VMEM((tm, GM_WIDTH), BF16),
                        pltpu.VMEM((tm, GM_WIDTH), BF16)],
        compiler_params=_params(1),
        name="gmlp_out",
    )(x, uvz, uvz, uvz, ln_g, ln_b, w_s, b_s_t, w_out, gate)


def _kv_kernel(x_ref, g_ref, sc_ref, sh_ref, wd_ref, gkva_ref, gkr_ref, wu_ref, gkn_ref,
               cos_ref, shi_ref, slo_ref, sgn_ref, k_ref, vt_ref, *, heads_per_chunk):
    h = _ada_norm(x_ref[...], g_ref[...], sc_ref[...], sh_ref[...])
    ckr = jnp.dot(h, wd_ref[...], preferred_element_type=F32)
    c_kv = (_rms(ckr[:, :KV_LORA_RANK], KV_LORA_RANK) * gkva_ref[...]).astype(BF16)
    k_r = _rms(ckr[:, KV_LORA_RANK:], QK_ROPE_DIM) * gkr_ref[...]
    k_r = _rope(k_r, cos_ref[...], shi_ref[...], slo_ref[...])
    k_x = (k_r + pltpu.roll(k_r, QK_ROPE_DIM, 1) * sgn_ref[...]).astype(BF16)
    for c0 in range(0, N_HEADS, heads_per_chunk):
        kv = jnp.dot(c_kv, wu_ref[:, c0 * HEAD_SLAB:(c0 + heads_per_chunk) * HEAD_SLAB],
                     preferred_element_type=F32)
        for i in range(heads_per_chunk):
            base = i * HEAD_SLAB
            k_n = _rms(kv[:, base:base + QK_NOPE_DIM], QK_NOPE_DIM) * gkn_ref[...]
            k_ref[c0 + i, :, :QK_NOPE_DIM] = k_n.astype(BF16)
            k_ref[c0 + i, :, QK_NOPE_DIM:] = k_x
            vt_ref[c0 + i] = kv[:, base + QK_NOPE_DIM:base + HEAD_SLAB].T.astype(BF16)


def _kv_proj(x, g, scale, shift, wd, gkva, gkr, wu, gkn, cos, s_hi, s_lo, sgn, batch, seq,
             tm=512, heads_per_chunk=2):
    t, d = x.shape
    per_batch = seq // tm
    vec = pl.BlockSpec((None, 1, d), lambda i: (i // per_batch, 0, 0))
    tab = pl.BlockSpec((tm, LANES), lambda i: (i % per_batch, 0))
    return pl.pallas_call(
        functools.partial(_kv_kernel, heads_per_chunk=heads_per_chunk),
        grid=(t // tm,),
        in_specs=[pl.BlockSpec((tm, d), lambda i: (i, 0)),
                  _resident((1, d)), vec, vec,
                  _resident(wd.shape), _resident(gkva.shape), _resident(gkr.shape),
                  _resident(wu.shape), _resident(gkn.shape), tab, tab, tab,
                  _resident(sgn.shape)],
        out_specs=[pl.BlockSpec((None, N_HEADS, tm, HEAD_SLAB),
                                lambda i: (i // per_batch, 0, i % per_batch, 0)),
                   pl.BlockSpec((None, N_HEADS, V_HEAD_DIM, tm),
                                lambda i: (i // per_batch, 0, 0, i % per_batch))],
        out_shape=[jax.ShapeDtypeStruct((batch, N_HEADS, seq, HEAD_SLAB), BF16),
                   jax.ShapeDtypeStruct((batch, N_HEADS, V_HEAD_DIM, seq), BF16)],
        compiler_params=_params(1),
        name="kv_proj",
    )(x, g, scale, shift, wd, gkva, gkr, wu, gkn, cos, s_hi, s_lo, sgn)


def _mla_in_kernel(x_ref, g_ref, sc_ref, sh_ref, win_ref, gqa_ref, wuq_ref, gqn_ref, gqr_ref,
                   tab_ref, q_ref, z_ref, *, chunk, heads_per_chunk):
    h = _ada_norm(x_ref[...], g_ref[...], sc_ref[...], sh_ref[...])
    c_q = jnp.dot(h, win_ref[:, :Q_LORA_RANK], preferred_element_type=F32)
    c_q = (_rms(c_q, Q_LORA_RANK) * gqa_ref[...]).astype(BF16)
    for n0 in range(0, ATTN_WIDTH, chunk):
        z = jnp.dot(h, win_ref[:, Q_LORA_RANK + n0:Q_LORA_RANK + n0 + chunk],
                    preferred_element_type=F32)
        z_ref[:, n0:n0 + chunk] = _silu(z).astype(BF16)
    rope_mult = tab_ref[...] * gqr_ref[...]
    for c0 in range(0, N_HEADS, heads_per_chunk):
        q = jnp.dot(c_q, wuq_ref[:, c0 * HEAD_SLAB:(c0 + heads_per_chunk) * HEAD_SLAB],
                    preferred_element_type=F32)
        for i in range(heads_per_chunk):
            q_n = q[:, i * HEAD_SLAB:i * HEAD_SLAB + QK_NOPE_DIM]
            q_x = q[:, i * HEAD_SLAB + QK_NOPE_DIM:(i + 1) * HEAD_SLAB]
            q_ref[c0 + i, :, :QK_NOPE_DIM] = (_rms(q_n, LANES) * gqn_ref[...]).astype(BF16)
            q_ref[c0 + i, :, QK_NOPE_DIM:] = (_rms(q_x, LANES) * rope_mult).astype(BF16)


def _mla_in(x, g, scale, shift, w_in, gqa, wuq, gqn, gqr, tab, batch, seq, tm=512, chunk=512,
            heads_per_chunk=2):
    t, d = x.shape
    per_batch = seq // tm
    vec = pl.BlockSpec((None, 1, d), lambda i: (i // per_batch, 0, 0))
    return pl.pallas_call(
        functools.partial(_mla_in_kernel, chunk=chunk, heads_per_chunk=heads_per_chunk),
        grid=(t // tm,),
        in_specs=[pl.BlockSpec((tm, d), lambda i: (i, 0)),
                  _resident((1, d)), vec, vec,
                  _resident(w_in.shape), _resident(gqa.shape), _resident(wuq.shape),
                  _resident(gqn.shape), _resident(gqr.shape),
                  pl.BlockSpec((tm, LANES), lambda i: (i % per_batch, 0))],
        out_specs=[pl.BlockSpec((None, N_HEADS, tm, HEAD_SLAB),
                                lambda i: (i // per_batch, 0, i % per_batch, 0)),
                   pl.BlockSpec((tm, ATTN_WIDTH), lambda i: (i, 0))],
        out_shape=[jax.ShapeDtypeStruct((batch, N_HEADS, seq, HEAD_SLAB), BF16),
                   jax.ShapeDtypeStruct((t, ATTN_WIDTH), BF16)],
        compiler_params=_params(1),
        name="mla_in",
    )(x, g, scale, shift, w_in, gqa, wuq, gqn, gqr, tab)


def _attn_kernel(q_ref, k_ref, vt_ref, o_ref, qt_ref, sa_ref, sb_ref, ma_ref, mb_ref, m_ref,
                 l_ref, acc_ref, *, tile):
    qi = pl.program_id(2)
    qt_ref[...] = q_ref[...].astype(F32).T.astype(BF16)

    def scores(j, s_ref, tile_max_ref, diagonal=False):
        k = k_ref[pl.ds(pl.multiple_of(j * tile, tile), tile), :]
        s = jnp.dot(k, qt_ref[...], preferred_element_type=F32)
        if diagonal:
            key = lax.broadcasted_iota(jnp.int32, s.shape, 0)
            query = lax.broadcasted_iota(jnp.int32, s.shape, 1)
            s = jnp.where(key <= query, s, -jnp.inf)
        s_ref[...] = s
        tile_max_ref[...] = jnp.max(s, axis=0, keepdims=True)

    def update(j, s_ref, tile_max_ref):
        vt = vt_ref[:, pl.ds(pl.multiple_of(j * tile, tile), tile)]
        m = m_ref[...]
        m_new = jnp.maximum(m, tile_max_ref[...])
        alpha = jnp.exp2(m - m_new)
        p = jnp.exp2(s_ref[...] - m_new)
        m_ref[...] = m_new
        l_ref[...] = alpha * l_ref[...] + jnp.sum(p, axis=0, keepdims=True)
        acc_ref[...] = alpha * acc_ref[...] + jnp.dot(vt, p.astype(BF16),
                                                      preferred_element_type=F32)

    def finish():
        o_ref[...] = (acc_ref[...] / l_ref[...]).T.astype(o_ref.dtype)

    m_ref[...] = jnp.full(m_ref.shape, -jnp.inf, F32)
    l_ref[...] = jnp.zeros(l_ref.shape, F32)
    acc_ref[...] = jnp.zeros(acc_ref.shape, F32)

    @pl.when(qi > 0)
    def _():
        scores(0, sa_ref, ma_ref)

    @pl.loop(0, jnp.maximum(qi - 1, 0) // 2)
    def _(jj):
        scores(2 * jj + 1, sb_ref, mb_ref)
        update(2 * jj, sa_ref, ma_ref)
        scores(2 * jj + 2, sa_ref, ma_ref)
        update(2 * jj + 1, sb_ref, mb_ref)

    @pl.when(jnp.logical_and(qi > 0, qi % 2 == 0))
    def _():
        scores(qi - 1, sb_ref, mb_ref)
        update(qi - 2, sa_ref, ma_ref)
        scores(qi, sa_ref, ma_ref, diagonal=True)
        update(qi - 1, sb_ref, mb_ref)
        update(qi, sa_ref, ma_ref)
        finish()

    @pl.when(qi % 2 == 1)
    def _():
        scores(qi, sb_ref, mb_ref, diagonal=True)
        update(qi - 1, sa_ref, ma_ref)
        update(qi, sb_ref, mb_ref)
        finish()

    @pl.when(qi == 0)
    def _():
        scores(0, sa_ref, ma_ref, diagonal=True)
        update(0, sa_ref, ma_ref)
        finish()


def _attention(q, k, vt, tile=1024):
    batch, heads, seq, width = q.shape
    return pl.pallas_call(
        functools.partial(_attn_kernel, tile=tile),
        grid=(batch, heads, seq // tile),
        in_specs=[pl.BlockSpec((None, None, tile, width), lambda b, h, i: (b, h, i, 0)),
                  pl.BlockSpec((None, None, seq, width), lambda b, h, i: (b, h, 0, 0)),
                  pl.BlockSpec((None, None, V_HEAD_DIM, seq), lambda b, h, i: (b, h, 0, 0))],
        out_specs=pl.BlockSpec((None, tile, V_HEAD_DIM), lambda b, h, i: (b, i, h)),
        out_shape=jax.ShapeDtypeStruct((batch, seq, heads * V_HEAD_DIM), BF16),
        scratch_shapes=[pltpu.VMEM((width, tile), BF16),
                        pltpu.VMEM((tile, tile), F32), pltpu.VMEM((tile, tile), F32),
                        pltpu.VMEM((1, tile), F32), pltpu.VMEM((1, tile), F32),
                        pltpu.VMEM((1, tile), F32), pltpu.VMEM((1, tile), F32),
                        pltpu.VMEM((V_HEAD_DIM, tile), F32)],
        compiler_params=_params(3),
        name="attention",
    )(q, k, vt)


def _mla_out_kernel(x_ref, a_ref, z_ref, w_ref, gate_ref, o_ref):
    y = (a_ref[...].astype(F32) * z_ref[...].astype(F32)).astype(BF16)
    out = jnp.dot(y, w_ref[...], preferred_element_type=F32)
    o_ref[...] = x_ref[...] + gate_ref[...] * out


def _mla_out(x, a, z, w_out, gate, seq, tm=512):
    t, d = x.shape
    per_batch = seq // tm
    row = lambda w: pl.BlockSpec((tm, w), lambda i: (i, 0))
    return pl.pallas_call(
        _mla_out_kernel,
        grid=(t // tm,),
        in_specs=[row(d), row(ATTN_WIDTH), row(ATTN_WIDTH), _resident(w_out.shape),
                  pl.BlockSpec((None, 1, d), lambda i: (i // per_batch, 0, 0))],
        out_specs=row(d),
        out_shape=jax.ShapeDtypeStruct((t, d), F32),
        compiler_params=_params(1),
        name="mla_out",
    )(x, a, z, w_out, gate)


def _rope_halves(r):
    return r[..., :HALF_ROPE], r[..., HALF_ROPE:]


def _query_slabs(w):
    k = w.shape[0]
    w = w.reshape(k, N_HEADS, QK_DIM)
    x1, x2 = _rope_halves(w[..., QK_NOPE_DIM:])
    w = jnp.concatenate([w[..., :QK_NOPE_DIM], x1, x2, x2, x1], axis=-1)
    return w.reshape(k, N_HEADS * HEAD_SLAB)


def _rope_tables(seq):
    pos = jnp.arange(seq, dtype=F32)
    inv_freq = ROPE_THETA ** (-jnp.arange(0, QK_ROPE_DIM, 2, dtype=F32) / QK_ROPE_DIM)
    ang = pos[:, None] * inv_freq[None, :]
    cos, sin = jnp.cos(ang), jnp.sin(ang)
    zero = jnp.zeros_like(cos)
    pad = jnp.zeros((seq, LANES - QK_ROPE_DIM), F32)
    key_tables = (jnp.concatenate([cos, cos, pad], axis=1),
                  jnp.concatenate([zero, sin, pad], axis=1),
                  jnp.concatenate([-sin, zero, pad], axis=1))
    query_table = jnp.concatenate([cos, cos, sin, sin], axis=1)
    return key_tables, query_table


def _pad_lanes(g):
    return jnp.pad(g, (0, LANES - g.shape[0])).reshape(1, LANES)


def kernel(x, c, ada_w, ada_b, norm_g, a_w_in, a_ln_g, a_ln_b, a_w_s, a_b_s, a_w_out, kv_ada_w, kv_ada_b, kv_norm_g, kv_w_dkv, kv_g_kva, kv_w_ukv, kv_g_kn, kv_g_kr, b_w_in, b_g_qa, b_w_uq, b_g_qn, b_g_qr, b_w_out):
    batch, seq, d = x.shape
    t = batch * seq
    sm_scale = QK_DIM ** -0.5 * math.log2(math.e)

    c_pad = jnp.pad(c, ((0, 8 - batch), (0, 0)))
    mod = _ada_mod(c_pad, ada_w, ada_b)[:, :batch]
    kv_mod = _ada_mod(c_pad, kv_ada_w[None], kv_ada_b[None])[0, :batch]
    vecs = lambda m, n: [v.reshape(batch, 1, d) for v in jnp.split(m, n, axis=-1)]

    (cos, s_hi, s_lo), q_tab = _rope_tables(seq)
    sgn = jnp.concatenate([jnp.zeros((QK_ROPE_DIM,), F32), -jnp.ones((HALF_ROPE,), F32),
                           jnp.ones((HALF_ROPE,), F32)]).reshape(1, LANES)
    xf = x.reshape(t, d)

    for i in range(N_A_LAYERS):
        shift, scale, gate = vecs(mod[i], 3)
        uvz = _gmlp_in(xf, norm_g[i].reshape(1, d), scale, shift, a_w_in[i].astype(BF16), seq)
        xf = _gmlp_out(xf, uvz, a_ln_g[i].reshape(1, GM_WIDTH), a_ln_b[i].reshape(1, GM_WIDTH),
                       a_w_s[i], a_b_s[i].T, a_w_out[i].astype(BF16), gate, seq)

    kv_shift, kv_scale = vecs(kv_mod, 2)
    wd = jnp.pad(kv_w_dkv, ((0, 0), (0, LANES - QK_ROPE_DIM))).astype(BF16)
    k_all, vt_all = _kv_proj(xf, kv_norm_g.reshape(1, d), kv_scale, kv_shift, wd,
                             kv_g_kva.reshape(1, KV_LORA_RANK), _pad_lanes(kv_g_kr),
                             kv_w_ukv.astype(BF16), kv_g_kn.reshape(1, QK_NOPE_DIM),
                             cos, s_hi, s_lo, sgn, batch, seq)

    for j in range(N_B_LAYERS):
        shift, scale, gate = vecs(mod[N_A_LAYERS + j], 3)
        g1, g2 = _rope_halves(b_g_qr[j] * sm_scale)
        q, z = _mla_in(xf, norm_g[N_A_LAYERS + j].reshape(1, d), scale, shift,
                       b_w_in[j].astype(BF16), b_g_qa[j].reshape(1, Q_LORA_RANK),
                       _query_slabs(b_w_uq[j]).astype(BF16),
                       (b_g_qn[j] * sm_scale).reshape(1, QK_NOPE_DIM),
                       jnp.concatenate([g1, g2, g2, g1]).reshape(1, LANES), q_tab, batch, seq)
        a = _attention(q, k_all, vt_all)
        xf = _mla_out(xf, a.reshape(t, ATTN_WIDTH), z, b_w_out[j].astype(BF16), gate, seq)

    return xf.reshape(batch, seq, d)
```

```python
import functools
import math

import jax
import jax.numpy as jnp
from jax import lax
from jax.experimental import pallas as pl
from jax.experimental.pallas import tpu as pltpu

F32 = jnp.float32
BF16 = jnp.bfloat16

D_MODEL = 2048
N_A_LAYERS = 2
N_B_LAYERS = 2
DEPTH = N_A_LAYERS + N_B_LAYERS
GM_WIDTH = 2 * D_MODEL
GM_GROUPS = 16
GM_GROUP_DIM = GM_WIDTH // GM_GROUPS
CHUNK = 128
N_HEADS = 16
QK_NOPE_DIM = 128
QK_ROPE_DIM = 64
HALF_ROPE = QK_ROPE_DIM // 2
QK_DIM = QK_NOPE_DIM + QK_ROPE_DIM
V_HEAD_DIM = 128
Q_LORA_RANK = 512
KV_LORA_RANK = 512
ATTN_WIDTH = N_HEADS * V_HEAD_DIM
ROPE_THETA = 10000.0
EPS = 1e-6

LANES = 128
HEAD_SLAB = 2 * LANES
VMEM_LIMIT = 56 * 1024 * 1024

GELU_C = math.sqrt(2.0 / math.pi)


def _silu(a):
    return a / (1.0 + jnp.exp(-a))


def _rms(a, width):
    return a * lax.rsqrt(jnp.sum(a * a, axis=-1, keepdims=True) * (1.0 / width) + EPS)


def _ada_norm(x, g, scale, shift):
    return (_rms(x, x.shape[-1]) * (g * (1.0 + scale)) + shift).astype(BF16)


def _rope(r, cos, sin_hi, sin_lo):
    return (r * cos + pltpu.roll(r, HALF_ROPE, 1) * sin_hi
            + pltpu.roll(r, LANES - HALF_ROPE, 1) * sin_lo)


def _params(n_axes, flags=None):
    return pltpu.CompilerParams(dimension_semantics=("arbitrary",) * n_axes,
                                vmem_limit_bytes=VMEM_LIMIT, flags=flags)


def _resident(shape):
    return pl.BlockSpec(shape, lambda *_: (0,) * len(shape), pipeline_mode=pl.Buffered(1))


def _ada_kernel(c_ref, w_ref, b_ref, o_ref):
    a = _silu(c_ref[...]).astype(BF16)
    o_ref[...] = jnp.dot(a, w_ref[...].astype(BF16), preferred_element_type=F32) + b_ref[...]


def _ada_mod(c_pad, w, b, tn=1024):
    n_layers, d, n = w.shape
    return pl.pallas_call(
        _ada_kernel,
        grid=(n_layers, n // tn),
        in_specs=[pl.BlockSpec(c_pad.shape, lambda l, j: (0, 0)),
                  pl.BlockSpec((None, d, tn), lambda l, j: (l, 0, j)),
                  pl.BlockSpec((None, 1, tn), lambda l, j: (l, 0, j))],
        out_specs=pl.BlockSpec((None, c_pad.shape[0], tn), lambda l, j: (l, 0, j)),
        out_shape=jax.ShapeDtypeStruct((n_layers, c_pad.shape[0], n), F32),
        compiler_params=_params(2),
        name="ada_mod",
    )(c_pad, w, b.reshape(n_layers, 1, n))


def _gmlp_in_kernel(x_ref, g_ref, sc_ref, sh_ref, w_ref, o_ref, h_ref, *, n_gelu_tiles, chunk):
    j = pl.program_id(1)

    @pl.when(j == 0)
    def _():
        h_ref[...] = _ada_norm(x_ref[...], g_ref[...], sc_ref[...], sh_ref[...])

    is_gelu = j < n_gelu_tiles
    c1 = jnp.where(is_gelu, 2.0 * GELU_C, 1.0).astype(F32)
    c3 = jnp.where(is_gelu, 2.0 * GELU_C * 0.044715, 0.0).astype(F32)
    h = h_ref[...]
    for n0 in range(0, o_ref.shape[1], chunk):
        a = jnp.dot(h, w_ref[:, n0:n0 + chunk], preferred_element_type=F32)
        gate = a * (c1 + c3 * (a * a))
        o_ref[:, n0:n0 + chunk] = (a / (1.0 + jnp.exp(-gate))).astype(BF16)


def _layer_block(w, layer, *block):
    n_axes = len(block)
    return pl.BlockSpec((None, *block), lambda *_: (layer,) + (0,) * n_axes,
                        pipeline_mode=pl.Buffered(1))


def _gmlp_in(x, g, scale, shift, w, layer, seq, tm=1024, tn=2048, chunk=512):
    t, d = x.shape
    n = w.shape[2]
    per_batch = seq // tm
    vec = pl.BlockSpec((None, 1, d), lambda i, j: (i // per_batch, 0, 0))
    return pl.pallas_call(
        functools.partial(_gmlp_in_kernel, n_gelu_tiles=2 * GM_WIDTH // tn, chunk=chunk),
        grid=(t // tm, n // tn),
        in_specs=[pl.BlockSpec((tm, d), lambda i, j: (i, 0)),
                  pl.BlockSpec((1, d), lambda i, j: (0, 0)),
                  vec, vec,
                  pl.BlockSpec((None, d, tn), lambda i, j: (layer, 0, j))],
        out_specs=pl.BlockSpec((tm, tn), lambda i, j: (i, j)),
        out_shape=jax.ShapeDtypeStruct((t, n), BF16),
        scratch_shapes=[pltpu.VMEM((tm, d), BF16)],
        compiler_params=_params(2),
        name="gmlp_in",
    )(x, g, scale, shift, w)


def _gmlp_out_kernel(x_ref, u_ref, v_ref, z_ref, lng_ref, lnb_ref, ws_ref, bst_ref, w_ref,
                     gate_ref, o_ref, wsm_ref, vn_ref, y0_ref, y1_ref, *, proj_chunk):
    step = pl.program_id(0)

    @pl.when(step == 0)
    def _():
        row = lax.broadcasted_iota(jnp.int32, (CHUNK, CHUNK), 0)
        col = lax.broadcasted_iota(jnp.int32, (CHUNK, CHUNK), 1)
        for g in range(GM_GROUPS):
            wsm_ref[g] = jnp.where(col <= row, ws_ref[g], 0.0).astype(BF16)
        y1_ref[...] = jnp.zeros(y1_ref.shape, BF16)

    def layer_norm():
        v = v_ref[...].astype(F32)
        mu = jnp.mean(v, axis=-1, keepdims=True)
        vc = v - mu
        var = jnp.mean(vc * vc, axis=-1, keepdims=True)
        vn_ref[...] = (vc * lax.rsqrt(var + EPS) * lng_ref[...] + lnb_ref[...]).astype(BF16)

    def mix(y_ref, c, g):
        rows = pl.ds(c * CHUNK, CHUNK)
        cols = pl.ds(g * GM_GROUP_DIM, GM_GROUP_DIM)
        mixed = jnp.dot(wsm_ref[g], vn_ref[rows, cols], preferred_element_type=F32)
        mixed = mixed + bst_ref[:, g:g + 1]
        y = u_ref[rows, cols].astype(F32) * mixed * z_ref[rows, cols].astype(F32)
        y_ref[rows, cols] = y.astype(BF16)

    def project(y_ref, n0):
        cols = pl.ds(n0, proj_chunk)
        out = jnp.dot(y_ref[...], w_ref[:, cols], preferred_element_type=F32)
        o_ref[:, cols] = x_ref[:, cols] + gate_ref[:, cols] * out

    def both(y_prev_ref, y_next_ref):
        pieces = [layer_norm] + [functools.partial(mix, y_next_ref, c, g)
                                 for c in range(x_ref.shape[0] // CHUNK)
                                 for g in range(GM_GROUPS)]
        n_proj = o_ref.shape[1] // proj_chunk
        per_chunk = -(-len(pieces) // n_proj)
        for i in range(n_proj):
            project(y_prev_ref, i * proj_chunk)
            for piece in pieces[i * per_chunk:(i + 1) * per_chunk]:
                piece()

    @pl.when(step % 2 == 0)
    def _():
        both(y1_ref, y0_ref)

    @pl.when(step % 2 == 1)
    def _():
        both(y0_ref, y1_ref)


def _gmlp_out(x, uvz, ln_g, ln_b, w_s, b_s_t, w_out, layer, gate, seq, tm=256, proj_chunk=256):
    t, d = x.shape
    per_batch = seq // tm
    n_tiles = t // tm
    part = lambda k: pl.BlockSpec((tm, GM_WIDTH), lambda s: (jnp.minimum(s, n_tiles - 1), k))
    prev = lambda s: jnp.maximum(s - 1, 0)
    return pl.pallas_call(
        functools.partial(_gmlp_out_kernel, proj_chunk=proj_chunk),
        grid=(n_tiles + 1,),
        in_specs=[pl.BlockSpec((tm, d), lambda s: (prev(s), 0)),
                  part(0), part(1), part(2),
                  _layer_block(ln_g, layer, 1, GM_WIDTH), _layer_block(ln_b, layer, 1, GM_WIDTH),
                  _layer_block(w_s, layer, *w_s.shape[1:]),
                  _layer_block(b_s_t, layer, *b_s_t.shape[1:]),
                  _layer_block(w_out, layer, *w_out.shape[1:]),
                  pl.BlockSpec((None, 1, d), lambda s: (prev(s) // per_batch, 0, 0))],
        out_specs=pl.BlockSpec((tm, d), lambda s: (prev(s), 0)),
        out_shape=jax.ShapeDtypeStruct((t, d), F32),
        scratch_shapes=[pltpu.VMEM((GM_GROUPS, CHUNK, CHUNK), BF16),
                        pltpu.VMEM((tm, GM_WIDTH), BF16),
                        pltpu.VMEM((tm, GM_WIDTH), BF16),
                        pltpu.VMEM((tm, GM_WIDTH), BF16)],
        compiler_params=_params(1),
        name="gmlp_out",
    )(x, uvz, uvz, uvz, ln_g, ln_b, w_s, b_s_t, w_out, gate)


def _rows_by_cols_t(a, b):
    return lax.dot_general(a, b, (((1,), (1,)), ((), ())), preferred_element_type=F32)


def _kv_kernel(x_ref, g_ref, sc_ref, sh_ref, wd_ref, gkva_ref, gkr_ref, wk_ref, wvt_ref, gkn_ref,
               cos_ref, shi_ref, slo_ref, sgn_ref, k_ref, vt_ref, *, heads_per_chunk):
    h = _ada_norm(x_ref[...], g_ref[...], sc_ref[...], sh_ref[...])
    ckr = jnp.dot(h, wd_ref[...], preferred_element_type=F32)
    c_kv = (_rms(ckr[:, :KV_LORA_RANK], KV_LORA_RANK) * gkva_ref[...]).astype(BF16)
    k_r = _rms(ckr[:, KV_LORA_RANK:], QK_ROPE_DIM) * gkr_ref[...]
    k_r = _rope(k_r, cos_ref[...], shi_ref[...], slo_ref[...])
    k_x = (k_r + pltpu.roll(k_r, QK_ROPE_DIM, 1) * sgn_ref[...]).astype(BF16)
    for c0 in range(0, N_HEADS, heads_per_chunk):
        heads = slice(c0 * LANES, (c0 + heads_per_chunk) * LANES)
        k_n = jnp.dot(c_kv, wk_ref[:, heads], preferred_element_type=F32)
        v_t = _rows_by_cols_t(wvt_ref[heads, :], c_kv)
        for i in range(heads_per_chunk):
            one = slice(i * LANES, (i + 1) * LANES)
            k_ref[c0 + i, :, :QK_NOPE_DIM] = (_rms(k_n[:, one], QK_NOPE_DIM)
                                              * gkn_ref[...]).astype(BF16)
            k_ref[c0 + i, :, QK_NOPE_DIM:] = k_x
            vt_ref[c0 + i] = v_t[one, :].astype(BF16)


def _kv_proj(x, g, scale, shift, wd, gkva, gkr, wk, wvt, gkn, cos, s_hi, s_lo, sgn, batch, seq,
             tm=512, heads_per_chunk=2):
    t, d = x.shape
    per_batch = seq // tm
    vec = pl.BlockSpec((None, 1, d), lambda i: (i // per_batch, 0, 0))
    tab = pl.BlockSpec((tm, LANES), lambda i: (i % per_batch, 0))
    return pl.pallas_call(
        functools.partial(_kv_kernel, heads_per_chunk=heads_per_chunk),
        grid=(t // tm,),
        in_specs=[pl.BlockSpec((tm, d), lambda i: (i, 0)),
                  _resident((1, d)), vec, vec,
                  _resident(wd.shape), _resident(gkva.shape), _resident(gkr.shape),
                  _resident(wk.shape), _resident(wvt.shape), _resident(gkn.shape),
                  tab, tab, tab, _resident(sgn.shape)],
        out_specs=[pl.BlockSpec((None, N_HEADS, tm, HEAD_SLAB),
                                lambda i: (i // per_batch, 0, i % per_batch, 0)),
                   pl.BlockSpec((None, N_HEADS, V_HEAD_DIM, tm),
                                lambda i: (i // per_batch, 0, 0, i % per_batch))],
        out_shape=[jax.ShapeDtypeStruct((batch, N_HEADS, seq, HEAD_SLAB), BF16),
                   jax.ShapeDtypeStruct((batch, N_HEADS, V_HEAD_DIM, seq), BF16)],
        compiler_params=_params(1),
        name="kv_proj",
    )(x, g, scale, shift, wd, gkva, gkr, wk, wvt, gkn, cos, s_hi, s_lo, sgn)


def _col_rms(a):
    return a * lax.rsqrt(jnp.mean(a * a, axis=0, keepdims=True) + EPS)


def _mla_in_kernel(x_ref, g_ref, sc_ref, sh_ref, win_ref, gqa_ref, wuqt_ref, gqn_ref, gqr_ref,
                   tabt_ref, qt_ref, z_ref, *, chunk, heads_per_chunk):
    h = _ada_norm(x_ref[...], g_ref[...], sc_ref[...], sh_ref[...])
    c_q = jnp.dot(h, win_ref[:, :Q_LORA_RANK], preferred_element_type=F32)
    c_q = (_rms(c_q, Q_LORA_RANK) * gqa_ref[...]).astype(BF16)
    for n0 in range(0, ATTN_WIDTH, chunk):
        z = jnp.dot(h, win_ref[:, Q_LORA_RANK + n0:Q_LORA_RANK + n0 + chunk],
                    preferred_element_type=F32)
        z_ref[:, n0:n0 + chunk] = _silu(z).astype(BF16)
    tm = x_ref.shape[0]
    nope_mult = jnp.broadcast_to(gqn_ref[...], (QK_NOPE_DIM, tm))
    rope_mult = tabt_ref[...] * gqr_ref[...]
    for c0 in range(0, N_HEADS, heads_per_chunk):
        q_t = _rows_by_cols_t(wuqt_ref[c0 * HEAD_SLAB:(c0 + heads_per_chunk) * HEAD_SLAB, :],
                              c_q)
        for i in range(heads_per_chunk):
            q_n = q_t[i * HEAD_SLAB:i * HEAD_SLAB + QK_NOPE_DIM, :]
            q_x = q_t[i * HEAD_SLAB + QK_NOPE_DIM:(i + 1) * HEAD_SLAB, :]
            qt_ref[c0 + i, :QK_NOPE_DIM, :] = (_col_rms(q_n) * nope_mult).astype(BF16)
            qt_ref[c0 + i, QK_NOPE_DIM:, :] = (_col_rms(q_x) * rope_mult).astype(BF16)


def _mla_in(x, g, scale, shift, w_in, gqa, wuqt, gqn, gqr, layer, tabt, batch, seq, tm=512,
            chunk=512, heads_per_chunk=2):
    t, d = x.shape
    per_batch = seq // tm
    vec = pl.BlockSpec((None, 1, d), lambda i: (i // per_batch, 0, 0))
    return pl.pallas_call(
        functools.partial(_mla_in_kernel, chunk=chunk, heads_per_chunk=heads_per_chunk),
        grid=(t // tm,),
        in_specs=[pl.BlockSpec((tm, d), lambda i: (i, 0)),
                  _resident((1, d)), vec, vec,
                  _layer_block(w_in, layer, *w_in.shape[1:]),
                  _layer_block(gqa, layer, *gqa.shape[1:]),
                  _layer_block(wuqt, layer, *wuqt.shape[1:]),
                  _layer_block(gqn, layer, *gqn.shape[1:]),
                  _layer_block(gqr, layer, *gqr.shape[1:]),
                  pl.BlockSpec((LANES, tm), lambda i: (0, i % per_batch))],
        out_specs=[pl.BlockSpec((None, N_HEADS, HEAD_SLAB, tm),
                                lambda i: (i // per_batch, 0, 0, i % per_batch)),
                   pl.BlockSpec((tm, ATTN_WIDTH), lambda i: (i, 0))],
        out_shape=[jax.ShapeDtypeStruct((batch, N_HEADS, HEAD_SLAB, seq), BF16),
                   jax.ShapeDtypeStruct((t, ATTN_WIDTH), BF16)],
        compiler_params=_params(1),
        name="mla_in",
    )(x, g, scale, shift, w_in, gqa, wuqt, gqn, gqr, tabt)


def _attn_kernel(qt_ref, k_ref, vt_ref, o_ref, sa_ref, sb_ref, ma_ref, mb_ref, m_ref, l_ref,
                 acc_ref, *, tile):
    qi = pl.program_id(2)

    def scores(j, s_ref, tile_max_ref, diagonal=False):
        k = k_ref[pl.ds(pl.multiple_of(j * tile, tile), tile), :]
        s = jnp.dot(k, qt_ref[...], preferred_element_type=F32)
        if diagonal:
            key = lax.broadcasted_iota(jnp.int32, s.shape, 0)
            query = lax.broadcasted_iota(jnp.int32, s.shape, 1)
            s = jnp.where(key <= query, s, -jnp.inf)
        s_ref[...] = s
        tile_max_ref[...] = jnp.max(s, axis=0, keepdims=True)

    def update(j, s_ref, tile_max_ref):
        vt = vt_ref[:, pl.ds(pl.multiple_of(j * tile, tile), tile)]
        m = m_ref[...]
        m_new = jnp.maximum(m, tile_max_ref[...])
        alpha = jnp.exp2(m - m_new)
        p = jnp.exp2(s_ref[...] - m_new)
        m_ref[...] = m_new
        l_ref[...] = alpha * l_ref[...] + jnp.sum(p, axis=0, keepdims=True)
        acc_ref[...] = alpha * acc_ref[...] + jnp.dot(vt, p.astype(BF16),
                                                      preferred_element_type=F32)

    def finish():
        o_ref[...] = (acc_ref[...] / l_ref[...]).T.astype(o_ref.dtype)

    m_ref[...] = jnp.full(m_ref.shape, -jnp.inf, F32)
    l_ref[...] = jnp.zeros(l_ref.shape, F32)
    acc_ref[...] = jnp.zeros(acc_ref.shape, F32)

    @pl.when(qi > 0)
    def _():
        scores(0, sa_ref, ma_ref)

    @pl.loop(0, jnp.maximum(qi - 1, 0) // 2)
    def _(jj):
        scores(2 * jj + 1, sb_ref, mb_ref)
        update(2 * jj, sa_ref, ma_ref)
        scores(2 * jj + 2, sa_ref, ma_ref)
        update(2 * jj + 1, sb_ref, mb_ref)

    @pl.when(jnp.logical_and(qi > 0, qi % 2 == 0))
    def _():
        scores(qi - 1, sb_ref, mb_ref)
        update(qi - 2, sa_ref, ma_ref)
        scores(qi, sa_ref, ma_ref, diagonal=True)
        update(qi - 1, sb_ref, mb_ref)
        update(qi, sa_ref, ma_ref)
        finish()

    @pl.when(qi % 2 == 1)
    def _():
        scores(qi, sb_ref, mb_ref, diagonal=True)
        update(qi - 1, sa_ref, ma_ref)
        update(qi, sb_ref, mb_ref)
        finish()

    @pl.when(qi == 0)
    def _():
        scores(0, sa_ref, ma_ref, diagonal=True)
        update(0, sa_ref, ma_ref)
        finish()


def _attention(qt, k, vt, tile=1024):
    batch, heads, width, seq = qt.shape
    return pl.pallas_call(
        functools.partial(_attn_kernel, tile=tile),
        grid=(batch, heads, seq // tile),
        in_specs=[pl.BlockSpec((None, None, width, tile), lambda b, h, i: (b, h, 0, i)),
                  pl.BlockSpec((None, None, seq, width), lambda b, h, i: (b, h, 0, 0)),
                  pl.BlockSpec((None, None, V_HEAD_DIM, seq), lambda b, h, i: (b, h, 0, 0))],
        out_specs=pl.BlockSpec((None, tile, V_HEAD_DIM), lambda b, h, i: (b, i, h)),
        out_shape=jax.ShapeDtypeStruct((batch, seq, heads * V_HEAD_DIM), BF16),
        scratch_shapes=[pltpu.VMEM((tile, tile), F32), pltpu.VMEM((tile, tile), F32),
                        pltpu.VMEM((1, tile), F32), pltpu.VMEM((1, tile), F32),
                        pltpu.VMEM((1, tile), F32), pltpu.VMEM((1, tile), F32),
                        pltpu.VMEM((V_HEAD_DIM, tile), F32)],
        compiler_params=_params(3),
        name="attention",
    )(qt, k, vt)


def _mla_out_kernel(x_ref, a_ref, z_ref, w_ref, gate_ref, o_ref):
    y = (a_ref[...].astype(F32) * z_ref[...].astype(F32)).astype(BF16)
    out = jnp.dot(y, w_ref[...], preferred_element_type=F32)
    o_ref[...] = x_ref[...] + gate_ref[...] * out


def _mla_out(x, a, z, w_out, layer, gate, seq, tm=512):
    t, d = x.shape
    per_batch = seq // tm
    row = lambda w: pl.BlockSpec((tm, w), lambda i: (i, 0))
    return pl.pallas_call(
        _mla_out_kernel,
        grid=(t // tm,),
        in_specs=[row(d), row(ATTN_WIDTH), row(ATTN_WIDTH),
                  _layer_block(w_out, layer, *w_out.shape[1:]),
                  pl.BlockSpec((None, 1, d), lambda i: (i // per_batch, 0, 0))],
        out_specs=row(d),
        out_shape=jax.ShapeDtypeStruct((t, d), F32),
        compiler_params=_params(1),
        name="mla_out",
    )(x, a, z, w_out, gate)


def _rope_halves(r):
    return r[..., :HALF_ROPE], r[..., HALF_ROPE:]


def _query_slabs(w):
    lead = w.shape[:-1]
    w = w.reshape(*lead, N_HEADS, QK_DIM)
    x1, x2 = _rope_halves(w[..., QK_NOPE_DIM:])
    w = jnp.concatenate([w[..., :QK_NOPE_DIM], x1, x2, x2, x1], axis=-1)
    return w.reshape(*lead, N_HEADS * HEAD_SLAB)


def _rope_tables(seq):
    pos = jnp.arange(seq, dtype=F32)
    inv_freq = ROPE_THETA ** (-jnp.arange(0, QK_ROPE_DIM, 2, dtype=F32) / QK_ROPE_DIM)
    ang = pos[:, None] * inv_freq[None, :]
    cos, sin = jnp.cos(ang), jnp.sin(ang)
    zero = jnp.zeros_like(cos)
    pad = jnp.zeros((seq, LANES - QK_ROPE_DIM), F32)
    key_tables = (jnp.concatenate([cos, cos, pad], axis=1),
                  jnp.concatenate([zero, sin, pad], axis=1),
                  jnp.concatenate([-sin, zero, pad], axis=1))
    query_table = jnp.concatenate([cos, cos, sin, sin], axis=1)
    return key_tables, query_table


def _pad_lanes(g):
    return jnp.pad(g, (0, LANES - g.shape[0])).reshape(1, LANES)


def kernel(x, c, ada_w, ada_b, norm_g, a_w_in, a_ln_g, a_ln_b, a_w_s, a_b_s, a_w_out, kv_ada_w, kv_ada_b, kv_norm_g, kv_w_dkv, kv_g_kva, kv_w_ukv, kv_g_kn, kv_g_kr, b_w_in, b_g_qa, b_w_uq, b_g_qn, b_g_qr, b_w_out):
    batch, seq, d = x.shape
    t = batch * seq
    sm_scale = QK_DIM ** -0.5 * math.log2(math.e)

    c_pad = jnp.pad(c, ((0, 8 - batch), (0, 0)))
    mod = _ada_mod(c_pad, ada_w, ada_b)[:, :batch]
    kv_mod = _ada_mod(c_pad, kv_ada_w[None], kv_ada_b[None])[0, :batch]
    vecs = lambda m, n: [v.reshape(batch, 1, d) for v in jnp.split(m, n, axis=-1)]

    (cos, s_hi, s_lo), q_tab = _rope_tables(seq)
    sgn = jnp.concatenate([jnp.zeros((QK_ROPE_DIM,), F32), -jnp.ones((HALF_ROPE,), F32),
                           jnp.ones((HALF_ROPE,), F32)]).reshape(1, LANES)
    xf = x.reshape(t, d)

    a_w_in_bf, a_w_out_bf = a_w_in.astype(BF16), a_w_out.astype(BF16)
    a_ln_g3 = a_ln_g.reshape(N_A_LAYERS, 1, GM_WIDTH)
    a_ln_b3 = a_ln_b.reshape(N_A_LAYERS, 1, GM_WIDTH)
    a_b_s_t = jnp.swapaxes(a_b_s, 1, 2)
    for i in range(N_A_LAYERS):
        shift, scale, gate = vecs(mod[i], 3)
        uvz = _gmlp_in(xf, norm_g[i].reshape(1, d), scale, shift, a_w_in_bf, i, seq)
        xf = _gmlp_out(xf, uvz, a_ln_g3, a_ln_b3, a_w_s, a_b_s_t, a_w_out_bf, i, gate, seq)

    kv_shift, kv_scale = vecs(kv_mod, 2)
    wd = jnp.pad(kv_w_dkv, ((0, 0), (0, LANES - QK_ROPE_DIM))).astype(BF16)
    w_ukv = kv_w_ukv.reshape(KV_LORA_RANK, N_HEADS, QK_NOPE_DIM + V_HEAD_DIM).astype(BF16)
    wk = w_ukv[..., :QK_NOPE_DIM].reshape(KV_LORA_RANK, N_HEADS * QK_NOPE_DIM)
    wvt = w_ukv[..., QK_NOPE_DIM:].reshape(KV_LORA_RANK, N_HEADS * V_HEAD_DIM).T
    k_all, vt_all = _kv_proj(xf, kv_norm_g.reshape(1, d), kv_scale, kv_shift, wd,
                             kv_g_kva.reshape(1, KV_LORA_RANK), _pad_lanes(kv_g_kr),
                             wk, wvt, kv_g_kn.reshape(1, QK_NOPE_DIM),
                             cos, s_hi, s_lo, sgn, batch, seq)

    b_w_in_bf, b_w_out_bf = b_w_in.astype(BF16), b_w_out.astype(BF16)
    b_w_uqt_bf = jnp.swapaxes(_query_slabs(b_w_uq), 1, 2).astype(BF16)
    b_g_qa3 = b_g_qa.reshape(N_B_LAYERS, 1, Q_LORA_RANK)
    b_g_qn3 = (b_g_qn * sm_scale).reshape(N_B_LAYERS, QK_NOPE_DIM, 1)
    g1, g2 = _rope_halves(b_g_qr * sm_scale)
    b_g_qr3 = jnp.concatenate([g1, g2, g2, g1], axis=-1).reshape(N_B_LAYERS, LANES, 1)
    q_tab_t = q_tab.T
    for j in range(N_B_LAYERS):
        shift, scale, gate = vecs(mod[N_A_LAYERS + j], 3)
        q_t, z = _mla_in(xf, norm_g[N_A_LAYERS + j].reshape(1, d), scale, shift, b_w_in_bf,
                         b_g_qa3, b_w_uqt_bf, b_g_qn3, b_g_qr3, j, q_tab_t, batch, seq)
        a = _attention(q_t, k_all, vt_all)
        xf = _mla_out(xf, a.reshape(t, ATTN_WIDTH), z, b_w_out_bf, j, gate, seq)

    return xf.reshape(batch, seq, d)
```

```python
import functools
import math

import jax
import jax.numpy as jnp
from jax import lax
from jax.experimental import pallas as pl
from jax.experimental.pallas import tpu as pltpu

F32 = jnp.float32
BF16 = jnp.bfloat16

D_MODEL = 2048
N_A_LAYERS = 2
N_B_LAYERS = 2
DEPTH = N_A_LAYERS + N_B_LAYERS
GM_WIDTH = 2 * D_MODEL
GM_GROUPS = 16
GM_GROUP_DIM = GM_WIDTH // GM_GROUPS
CHUNK = 128
N_HEADS = 16
QK_NOPE_DIM = 128
QK_ROPE_DIM = 64
HALF_ROPE = QK_ROPE_DIM // 2
QK_DIM = QK_NOPE_DIM + QK_ROPE_DIM
V_HEAD_DIM = 128
Q_LORA_RANK = 512
KV_LORA_RANK = 512
ATTN_WIDTH = N_HEADS * V_HEAD_DIM
ROPE_THETA = 10000.0
EPS = 1e-6

LANES = 128
HEAD_SLAB = 2 * LANES
VMEM_LIMIT = 56 * 1024 * 1024

GELU_C = math.sqrt(2.0 / math.pi)


def _silu(a):
    return a / (1.0 + jnp.exp(-a))


def _rms(a, width):
    return a * lax.rsqrt(jnp.sum(a * a, axis=-1, keepdims=True) * (1.0 / width) + EPS)


def _ada_norm(x, g, scale, shift):
    return (_rms(x, x.shape[-1]) * (g * (1.0 + scale)) + shift).astype(BF16)


def _rope(r, cos, sin_hi, sin_lo):
    return (r * cos + pltpu.roll(r, HALF_ROPE, 1) * sin_hi
            + pltpu.roll(r, LANES - HALF_ROPE, 1) * sin_lo)


def _params(n_axes, flags=None):
    return pltpu.CompilerParams(dimension_semantics=("arbitrary",) * n_axes,
                                vmem_limit_bytes=VMEM_LIMIT, flags=flags)


def _resident(shape):
    return pl.BlockSpec(shape, lambda *_: (0,) * len(shape), pipeline_mode=pl.Buffered(1))


def _ada_kernel(c_ref, w_ref, b_ref, o_ref):
    a = _silu(c_ref[...]).astype(BF16)
    o_ref[...] = jnp.dot(a, w_ref[...].astype(BF16), preferred_element_type=F32) + b_ref[...]


def _ada_mod(c_pad, w, b, tn=1024):
    n_layers, d, n = w.shape
    return pl.pallas_call(
        _ada_kernel,
        grid=(n_layers, n // tn),
        in_specs=[pl.BlockSpec(c_pad.shape, lambda l, j: (0, 0)),
                  pl.BlockSpec((None, d, tn), lambda l, j: (l, 0, j)),
                  pl.BlockSpec((None, 1, tn), lambda l, j: (l, 0, j))],
        out_specs=pl.BlockSpec((None, c_pad.shape[0], tn), lambda l, j: (l, 0, j)),
        out_shape=jax.ShapeDtypeStruct((n_layers, c_pad.shape[0], n), F32),
        compiler_params=_params(2),
        name="ada_mod",
    )(c_pad, w, b.reshape(n_layers, 1, n))


def _gmlp_in_kernel(x_ref, g_ref, sc_ref, sh_ref, w_ref, o_ref, h_ref, *, n_gelu_tiles, chunk):
    j = pl.program_id(1)

    @pl.when(j == 0)
    def _():
        h_ref[...] = _ada_norm(x_ref[...], g_ref[...], sc_ref[...], sh_ref[...])

    is_gelu = j < n_gelu_tiles
    c1 = jnp.where(is_gelu, 2.0 * GELU_C, 1.0).astype(F32)
    c3 = jnp.where(is_gelu, 2.0 * GELU_C * 0.044715, 0.0).astype(F32)
    h = h_ref[...]
    for n0 in range(0, o_ref.shape[1], chunk):
        a = jnp.dot(h, w_ref[:, n0:n0 + chunk], preferred_element_type=F32)
        gate = a * (c1 + c3 * (a * a))
        o_ref[:, n0:n0 + chunk] = (a / (1.0 + jnp.exp(-gate))).astype(BF16)


def _layer_block(w, layer, *block):
    n_axes = len(block)
    return pl.BlockSpec((None, *block), lambda *_: (layer,) + (0,) * n_axes,
                        pipeline_mode=pl.Buffered(1))


def _gmlp_in(x, g, scale, shift, w, layer, seq, tm=1024, tn=2048, chunk=512):
    t, d = x.shape
    n = w.shape[2]
    per_batch = seq // tm
    vec = pl.BlockSpec((None, 1, d), lambda i, j: (i // per_batch, 0, 0))
    return pl.pallas_call(
        functools.partial(_gmlp_in_kernel, n_gelu_tiles=2 * GM_WIDTH // tn, chunk=chunk),
        grid=(t // tm, n // tn),
        in_specs=[pl.BlockSpec((tm, d), lambda i, j: (i, 0)),
                  pl.BlockSpec((1, d), lambda i, j: (0, 0)),
                  vec, vec,
                  pl.BlockSpec((None, d, tn), lambda i, j: (layer, 0, j))],
        out_specs=pl.BlockSpec((tm, tn), lambda i, j: (i, j)),
        out_shape=jax.ShapeDtypeStruct((t, n), BF16),
        scratch_shapes=[pltpu.VMEM((tm, d), BF16)],
        compiler_params=_params(2),
        name="gmlp_in",
    )(x, g, scale, shift, w)


def _gmlp_out_kernel(x_ref, u_ref, v_ref, z_ref, lng_ref, lnb_ref, ws_ref, bst_ref, w_ref,
                     gate_ref, o_ref, wsm_ref, vn_ref, y0_ref, y1_ref, *, proj_chunk):
    step = pl.program_id(0)

    @pl.when(step == 0)
    def _():
        row = lax.broadcasted_iota(jnp.int32, (CHUNK, CHUNK), 0)
        col = lax.broadcasted_iota(jnp.int32, (CHUNK, CHUNK), 1)
        for g in range(GM_GROUPS):
            wsm_ref[g] = jnp.where(col <= row, ws_ref[g], 0.0).astype(BF16)
        y1_ref[...] = jnp.zeros(y1_ref.shape, BF16)

    def layer_norm():
        v = v_ref[...].astype(F32)
        mu = jnp.mean(v, axis=-1, keepdims=True)
        vc = v - mu
        var = jnp.mean(vc * vc, axis=-1, keepdims=True)
        vn_ref[...] = (vc * lax.rsqrt(var + EPS) * lng_ref[...] + lnb_ref[...]).astype(BF16)

    def mix(y_ref, c, g):
        rows = pl.ds(c * CHUNK, CHUNK)
        cols = pl.ds(g * GM_GROUP_DIM, GM_GROUP_DIM)
        mixed = jnp.dot(wsm_ref[g], vn_ref[rows, cols], preferred_element_type=F32)
        mixed = mixed + bst_ref[:, g:g + 1]
        y = u_ref[rows, cols].astype(F32) * mixed * z_ref[rows, cols].astype(F32)
        y_ref[rows, cols] = y.astype(BF16)

    def project(y_ref, n0):
        cols = pl.ds(n0, proj_chunk)
        out = jnp.dot(y_ref[...], w_ref[:, cols], preferred_element_type=F32)
        o_ref[:, cols] = x_ref[:, cols] + gate_ref[:, cols] * out

    def both(y_prev_ref, y_next_ref):
        pieces = [layer_norm] + [functools.partial(mix, y_next_ref, c, g)
                                 for c in range(x_ref.shape[0] // CHUNK)
                                 for g in range(GM_GROUPS)]
        n_proj = o_ref.shape[1] // proj_chunk
        per_chunk = -(-len(pieces) // n_proj)
        for i in range(n_proj):
            project(y_prev_ref, i * proj_chunk)
            for piece in pieces[i * per_chunk:(i + 1) * per_chunk]:
                piece()

    @pl.when(step % 2 == 0)
    def _():
        both(y1_ref, y0_ref)

    @pl.when(step % 2 == 1)
    def _():
        both(y0_ref, y1_ref)


def _gmlp_out(x, uvz, ln_g, ln_b, w_s, b_s_t, w_out, layer, gate, seq, tm=256, proj_chunk=256):
    t, d = x.shape
    per_batch = seq // tm
    n_tiles = t // tm
    part = lambda k: pl.BlockSpec((tm, GM_WIDTH), lambda s: (jnp.minimum(s, n_tiles - 1), k))
    prev = lambda s: jnp.maximum(s - 1, 0)
    return pl.pallas_call(
        functools.partial(_gmlp_out_kernel, proj_chunk=proj_chunk),
        grid=(n_tiles + 1,),
        in_specs=[pl.BlockSpec((tm, d), lambda s: (prev(s), 0)),
                  part(0), part(1), part(2),
                  _layer_block(ln_g, layer, 1, GM_WIDTH), _layer_block(ln_b, layer, 1, GM_WIDTH),
                  _layer_block(w_s, layer, *w_s.shape[1:]),
                  _layer_block(b_s_t, layer, *b_s_t.shape[1:]),
                  _layer_block(w_out, layer, *w_out.shape[1:]),
                  pl.BlockSpec((None, 1, d), lambda s: (prev(s) // per_batch, 0, 0))],
        out_specs=pl.BlockSpec((tm, d), lambda s: (prev(s), 0)),
        out_shape=jax.ShapeDtypeStruct((t, d), F32),
        scratch_shapes=[pltpu.VMEM((GM_GROUPS, CHUNK, CHUNK), BF16),
                        pltpu.VMEM((tm, GM_WIDTH), BF16),
                        pltpu.VMEM((tm, GM_WIDTH), BF16),
                        pltpu.VMEM((tm, GM_WIDTH), BF16)],
        compiler_params=_params(1),
        name="gmlp_out",
    )(x, uvz, uvz, uvz, ln_g, ln_b, w_s, b_s_t, w_out, gate)


def _rows_by_cols_t(a, b):
    return lax.dot_general(a, b, (((1,), (1,)), ((), ())), preferred_element_type=F32)


def _kv_kernel(x_ref, g_ref, sc_ref, sh_ref, wd_ref, gkva_ref, gkr_ref, wk_ref, wvt_ref, gkn_ref,
               cos_ref, shi_ref, slo_ref, sgn_ref, k_ref, vt_ref, *, heads_per_chunk):
    h = _ada_norm(x_ref[...], g_ref[...], sc_ref[...], sh_ref[...])
    ckr = jnp.dot(h, wd_ref[...], preferred_element_type=F32)
    c_kv = (_rms(ckr[:, :KV_LORA_RANK], KV_LORA_RANK) * gkva_ref[...]).astype(BF16)
    k_r = _rms(ckr[:, KV_LORA_RANK:], QK_ROPE_DIM) * gkr_ref[...]
    k_r = _rope(k_r, cos_ref[...], shi_ref[...], slo_ref[...])
    k_x = (k_r + pltpu.roll(k_r, QK_ROPE_DIM, 1) * sgn_ref[...]).astype(BF16)
    for c0 in range(0, N_HEADS, heads_per_chunk):
        heads = slice(c0 * LANES, (c0 + heads_per_chunk) * LANES)
        k_n = jnp.dot(c_kv, wk_ref[:, heads], preferred_element_type=F32)
        v_t = _rows_by_cols_t(wvt_ref[heads, :], c_kv)
        for i in range(heads_per_chunk):
            one = slice(i * LANES, (i + 1) * LANES)
            k_ref[c0 + i, :, :QK_NOPE_DIM] = (_rms(k_n[:, one], QK_NOPE_DIM)
                                              * gkn_ref[...]).astype(BF16)
            k_ref[c0 + i, :, QK_NOPE_DIM:] = k_x
            vt_ref[c0 + i] = v_t[one, :].astype(BF16)


def _kv_proj(x, g, scale, shift, wd, gkva, gkr, wk, wvt, gkn, cos, s_hi, s_lo, sgn, batch, seq,
             tm=512, heads_per_chunk=2):
    t, d = x.shape
    per_batch = seq // tm
    vec = pl.BlockSpec((None, 1, d), lambda i: (i // per_batch, 0, 0))
    tab = pl.BlockSpec((tm, LANES), lambda i: (i % per_batch, 0))
    return pl.pallas_call(
        functools.partial(_kv_kernel, heads_per_chunk=heads_per_chunk),
        grid=(t // tm,),
        in_specs=[pl.BlockSpec((tm, d), lambda i: (i, 0)),
                  _resident((1, d)), vec, vec,
                  _resident(wd.shape), _resident(gkva.shape), _resident(gkr.shape),
                  _resident(wk.shape), _resident(wvt.shape), _resident(gkn.shape),
                  tab, tab, tab, _resident(sgn.shape)],
        out_specs=[pl.BlockSpec((None, N_HEADS, tm, HEAD_SLAB),
                                lambda i: (i // per_batch, 0, i % per_batch, 0)),
                   pl.BlockSpec((None, N_HEADS, V_HEAD_DIM, tm),
                                lambda i: (i // per_batch, 0, 0, i % per_batch))],
        out_shape=[jax.ShapeDtypeStruct((batch, N_HEADS, seq, HEAD_SLAB), BF16),
                   jax.ShapeDtypeStruct((batch, N_HEADS, V_HEAD_DIM, seq), BF16)],
        compiler_params=_params(1),
        name="kv_proj",
    )(x, g, scale, shift, wd, gkva, gkr, wk, wvt, gkn, cos, s_hi, s_lo, sgn)


def _col_rms(a):
    return a * lax.rsqrt(jnp.mean(a * a, axis=0, keepdims=True) + EPS)


def _mla_in_kernel(x_ref, g_ref, sc_ref, sh_ref, win_ref, gqa_ref, wuqt_ref, gqn_ref, gqr_ref,
                   tabt_ref, qt_ref, z_ref, *, chunk, heads_per_chunk):
    h = _ada_norm(x_ref[...], g_ref[...], sc_ref[...], sh_ref[...])
    c_q = jnp.dot(h, win_ref[:, :Q_LORA_RANK], preferred_element_type=F32)
    c_q = (_rms(c_q, Q_LORA_RANK) * gqa_ref[...]).astype(BF16)
    for n0 in range(0, ATTN_WIDTH, chunk):
        z = jnp.dot(h, win_ref[:, Q_LORA_RANK + n0:Q_LORA_RANK + n0 + chunk],
                    preferred_element_type=F32)
        z_ref[:, n0:n0 + chunk] = _silu(z).astype(BF16)
    tm = x_ref.shape[0]
    nope_mult = jnp.broadcast_to(gqn_ref[...], (QK_NOPE_DIM, tm))
    rope_mult = tabt_ref[...] * gqr_ref[...]
    for c0 in range(0, N_HEADS, heads_per_chunk):
        q_t = _rows_by_cols_t(wuqt_ref[c0 * HEAD_SLAB:(c0 + heads_per_chunk) * HEAD_SLAB, :],
                              c_q)
        for i in range(heads_per_chunk):
            q_n = q_t[i * HEAD_SLAB:i * HEAD_SLAB + QK_NOPE_DIM, :]
            q_x = q_t[i * HEAD_SLAB + QK_NOPE_DIM:(i + 1) * HEAD_SLAB, :]
            qt_ref[c0 + i, :QK_NOPE_DIM, :] = (_col_rms(q_n) * nope_mult).astype(BF16)
            qt_ref[c0 + i, QK_NOPE_DIM:, :] = (_col_rms(q_x) * rope_mult).astype(BF16)


def _mla_in(x, g, scale, shift, w_in, gqa, wuqt, gqn, gqr, layer, tabt, batch, seq, tm=512,
            chunk=512, heads_per_chunk=2):
    t, d = x.shape
    per_batch = seq // tm
    vec = pl.BlockSpec((None, 1, d), lambda i: (i // per_batch, 0, 0))
    return pl.pallas_call(
        functools.partial(_mla_in_kernel, chunk=chunk, heads_per_chunk=heads_per_chunk),
        grid=(t // tm,),
        in_specs=[pl.BlockSpec((tm, d), lambda i: (i, 0)),
                  _resident((1, d)), vec, vec,
                  _layer_block(w_in, layer, *w_in.shape[1:]),
                  _layer_block(gqa, layer, *gqa.shape[1:]),
                  _layer_block(wuqt, layer, *wuqt.shape[1:]),
                  _layer_block(gqn, layer, *gqn.shape[1:]),
                  _layer_block(gqr, layer, *gqr.shape[1:]),
                  pl.BlockSpec((LANES, tm), lambda i: (0, i % per_batch))],
        out_specs=[pl.BlockSpec((None, N_HEADS, HEAD_SLAB, tm),
                                lambda i: (i // per_batch, 0, 0, i % per_batch)),
                   pl.BlockSpec((tm, ATTN_WIDTH), lambda i: (i, 0))],
        out_shape=[jax.ShapeDtypeStruct((batch, N_HEADS, HEAD_SLAB, seq), BF16),
                   jax.ShapeDtypeStruct((t, ATTN_WIDTH), BF16)],
        compiler_params=_params(1),
        name="mla_in",
    )(x, g, scale, shift, w_in, gqa, wuqt, gqn, gqr, tabt)


def _attn_kernel(qt_ref, k_ref, vt_ref, o_ref, sa_ref, sb_ref, ma_ref, mb_ref, m_ref, l_ref,
                 acc_ref, *, tile):
    qi = pl.program_id(2)

    half = tile // 2

    def causal(s):
        key = lax.broadcasted_iota(jnp.int32, s.shape, 0)
        query = lax.broadcasted_iota(jnp.int32, s.shape, 1)
        return jnp.where(key <= query, s, -jnp.inf)

    def scores(j, s_ref, tile_max_ref):
        k = k_ref[pl.ds(pl.multiple_of(j * tile, tile), tile), :]
        s = jnp.dot(k, qt_ref[...], preferred_element_type=F32)
        s_ref[...] = s
        tile_max_ref[...] = jnp.max(s, axis=0, keepdims=True)

    def update(j, s_ref, tile_max_ref):
        vt = vt_ref[:, pl.ds(pl.multiple_of(j * tile, tile), tile)]
        m = m_ref[...]
        m_new = jnp.maximum(m, tile_max_ref[...])
        alpha = jnp.exp2(m - m_new)
        p = jnp.exp2(s_ref[...] - m_new)
        m_ref[...] = m_new
        l_ref[...] = alpha * l_ref[...] + jnp.sum(p, axis=0, keepdims=True)
        acc_ref[...] = alpha * acc_ref[...] + jnp.dot(vt, p.astype(BF16),
                                                      preferred_element_type=F32)

    def scores_diagonal(j, s_ref, tile_max_ref):
        start = pl.multiple_of(j * tile, tile)
        top = causal(jnp.dot(k_ref[pl.ds(start, half), :], qt_ref[...],
                             preferred_element_type=F32))
        bottom = causal(jnp.dot(k_ref[pl.ds(start + half, half), :], qt_ref[:, half:],
                                preferred_element_type=F32))
        s_ref[:half, :] = top
        s_ref[half:, half:] = bottom
        top_max = jnp.max(top, axis=0, keepdims=True)
        tile_max_ref[:, :half] = top_max[:, :half]
        tile_max_ref[:, half:] = jnp.maximum(top_max[:, half:],
                                             jnp.max(bottom, axis=0, keepdims=True))

    def update_diagonal(j, s_ref, tile_max_ref):
        start = pl.multiple_of(j * tile, tile)
        m = m_ref[...]
        m_new = jnp.maximum(m, tile_max_ref[...])
        alpha = jnp.exp2(m - m_new)
        p_top = jnp.exp2(s_ref[:half, :] - m_new)
        p_bottom = jnp.exp2(s_ref[half:, half:] - m_new[:, half:])
        m_ref[...] = m_new
        top_sum = jnp.sum(p_top, axis=0, keepdims=True)
        pv_top = jnp.dot(vt_ref[:, pl.ds(start, half)], p_top.astype(BF16),
                         preferred_element_type=F32)
        pv_bottom = jnp.dot(vt_ref[:, pl.ds(start + half, half)], p_bottom.astype(BF16),
                            preferred_element_type=F32)
        l_ref[:, :half] = alpha[:, :half] * l_ref[:, :half] + top_sum[:, :half]
        l_ref[:, half:] = (alpha[:, half:] * l_ref[:, half:] + top_sum[:, half:]
                           + jnp.sum(p_bottom, axis=0, keepdims=True))
        acc_ref[:, :half] = alpha[:, :half] * acc_ref[:, :half] + pv_top[:, :half]
        acc_ref[:, half:] = (alpha[:, half:] * acc_ref[:, half:] + pv_top[:, half:]
                             + pv_bottom)

    def finish():
        o_ref[...] = (acc_ref[...] / l_ref[...]).T.astype(o_ref.dtype)

    m_ref[...] = jnp.full(m_ref.shape, -jnp.inf, F32)
    l_ref[...] = jnp.zeros(l_ref.shape, F32)
    acc_ref[...] = jnp.zeros(acc_ref.shape, F32)

    @pl.when(qi > 0)
    def _():
        scores(0, sa_ref, ma_ref)

    @pl.loop(0, jnp.maximum(qi - 1, 0) // 2)
    def _(jj):
        scores(2 * jj + 1, sb_ref, mb_ref)
        update(2 * jj, sa_ref, ma_ref)
        scores(2 * jj + 2, sa_ref, ma_ref)
        update(2 * jj + 1, sb_ref, mb_ref)

    @pl.when(jnp.logical_and(qi > 0, qi % 2 == 0))
    def _():
        scores(qi - 1, sb_ref, mb_ref)
        update(qi - 2, sa_ref, ma_ref)
        scores_diagonal(qi, sa_ref, ma_ref)
        update(qi - 1, sb_ref, mb_ref)
        update_diagonal(qi, sa_ref, ma_ref)
        finish()

    @pl.when(qi % 2 == 1)
    def _():
        scores_diagonal(qi, sb_ref, mb_ref)
        update(qi - 1, sa_ref, ma_ref)
        update_diagonal(qi, sb_ref, mb_ref)
        finish()

    @pl.when(qi == 0)
    def _():
        scores_diagonal(0, sa_ref, ma_ref)
        update_diagonal(0, sa_ref, ma_ref)
        finish()


def _attention(qt, k, vt, tile=1024):
    batch, heads, width, seq = qt.shape
    return pl.pallas_call(
        functools.partial(_attn_kernel, tile=tile),
        grid=(batch, heads, seq // tile),
        in_specs=[pl.BlockSpec((None, None, width, tile), lambda b, h, i: (b, h, 0, i)),
                  pl.BlockSpec((None, None, seq, width), lambda b, h, i: (b, h, 0, 0)),
                  pl.BlockSpec((None, None, V_HEAD_DIM, seq), lambda b, h, i: (b, h, 0, 0))],
        out_specs=pl.BlockSpec((None, tile, V_HEAD_DIM), lambda b, h, i: (b, i, h)),
        out_shape=jax.ShapeDtypeStruct((batch, seq, heads * V_HEAD_DIM), BF16),
        scratch_shapes=[pltpu.VMEM((tile, tile), F32), pltpu.VMEM((tile, tile), F32),
                        pltpu.VMEM((1, tile), F32), pltpu.VMEM((1, tile), F32),
                        pltpu.VMEM((1, tile), F32), pltpu.VMEM((1, tile), F32),
                        pltpu.VMEM((V_HEAD_DIM, tile), F32)],
        compiler_params=_params(3),
        name="attention",
    )(qt, k, vt)


def _mla_out_kernel(x_ref, a_ref, z_ref, w_ref, gate_ref, o_ref):
    y = (a_ref[...].astype(F32) * z_ref[...].astype(F32)).astype(BF16)
    out = jnp.dot(y, w_ref[...], preferred_element_type=F32)
    o_ref[...] = x_ref[...] + gate_ref[...] * out


def _mla_out(x, a, z, w_out, layer, gate, seq, tm=512):
    t, d = x.shape
    per_batch = seq // tm
    row = lambda w: pl.BlockSpec((tm, w), lambda i: (i, 0))
    return pl.pallas_call(
        _mla_out_kernel,
        grid=(t // tm,),
        in_specs=[row(d), row(ATTN_WIDTH), row(ATTN_WIDTH),
                  _layer_block(w_out, layer, *w_out.shape[1:]),
                  pl.BlockSpec((None, 1, d), lambda i: (i // per_batch, 0, 0))],
        out_specs=row(d),
        out_shape=jax.ShapeDtypeStruct((t, d), F32),
        compiler_params=_params(1),
        name="mla_out",
    )(x, a, z, w_out, gate)


def _rope_halves(r):
    return r[..., :HALF_ROPE], r[..., HALF_ROPE:]


def _query_slabs(w):
    lead = w.shape[:-1]
    w = w.reshape(*lead, N_HEADS, QK_DIM)
    x1, x2 = _rope_halves(w[..., QK_NOPE_DIM:])
    w = jnp.concatenate([w[..., :QK_NOPE_DIM], x1, x2, x2, x1], axis=-1)
    return w.reshape(*lead, N_HEADS * HEAD_SLAB)


def _rope_tables(seq):
    pos = jnp.arange(seq, dtype=F32)
    inv_freq = ROPE_THETA ** (-jnp.arange(0, QK_ROPE_DIM, 2, dtype=F32) / QK_ROPE_DIM)
    ang = pos[:, None] * inv_freq[None, :]
    cos, sin = jnp.cos(ang), jnp.sin(ang)
    zero = jnp.zeros_like(cos)
    pad = jnp.zeros((seq, LANES - QK_ROPE_DIM), F32)
    key_tables = (jnp.concatenate([cos, cos, pad], axis=1),
                  jnp.concatenate([zero, sin, pad], axis=1),
                  jnp.concatenate([-sin, zero, pad], axis=1))
    query_table = jnp.concatenate([cos, cos, sin, sin], axis=1)
    return key_tables, query_table


def _pad_lanes(g):
    return jnp.pad(g, (0, LANES - g.shape[0])).reshape(1, LANES)


def kernel(x, c, ada_w, ada_b, norm_g, a_w_in, a_ln_g, a_ln_b, a_w_s, a_b_s, a_w_out, kv_ada_w, kv_ada_b, kv_norm_g, kv_w_dkv, kv_g_kva, kv_w_ukv, kv_g_kn, kv_g_kr, b_w_in, b_g_qa, b_w_uq, b_g_qn, b_g_qr, b_w_out):
    batch, seq, d = x.shape
    t = batch * seq
    sm_scale = QK_DIM ** -0.5 * math.log2(math.e)

    c_pad = jnp.pad(c, ((0, 8 - batch), (0, 0)))
    mod = _ada_mod(c_pad, ada_w, ada_b)[:, :batch]
    kv_mod = _ada_mod(c_pad, kv_ada_w[None], kv_ada_b[None])[0, :batch]
    vecs = lambda m, n: [v.reshape(batch, 1, d) for v in jnp.split(m, n, axis=-1)]

    (cos, s_hi, s_lo), q_tab = _rope_tables(seq)
    sgn = jnp.concatenate([jnp.zeros((QK_ROPE_DIM,), F32), -jnp.ones((HALF_ROPE,), F32),
                           jnp.ones((HALF_ROPE,), F32)]).reshape(1, LANES)
    xf = x.reshape(t, d)

    a_w_in_bf, a_w_out_bf = a_w_in.astype(BF16), a_w_out.astype(BF16)
    a_ln_g3 = a_ln_g.reshape(N_A_LAYERS, 1, GM_WIDTH)
    a_ln_b3 = a_ln_b.reshape(N_A_LAYERS, 1, GM_WIDTH)
    a_b_s_t = jnp.swapaxes(a_b_s, 1, 2)
    for i in range(N_A_LAYERS):
        shift, scale, gate = vecs(mod[i], 3)
        uvz = _gmlp_in(xf, norm_g[i].reshape(1, d), scale, shift, a_w_in_bf, i, seq)
        xf = _gmlp_out(xf, uvz, a_ln_g3, a_ln_b3, a_w_s, a_b_s_t, a_w_out_bf, i, gate, seq)

    kv_shift, kv_scale = vecs(kv_mod, 2)
    wd = jnp.pad(kv_w_dkv, ((0, 0), (0, LANES - QK_ROPE_DIM))).astype(BF16)
    w_ukv = kv_w_ukv.reshape(KV_LORA_RANK, N_HEADS, QK_NOPE_DIM + V_HEAD_DIM).astype(BF16)
    wk = w_ukv[..., :QK_NOPE_DIM].reshape(KV_LORA_RANK, N_HEADS * QK_NOPE_DIM)
    wvt = w_ukv[..., QK_NOPE_DIM:].reshape(KV_LORA_RANK, N_HEADS * V_HEAD_DIM).T
    k_all, vt_all = _kv_proj(xf, kv_norm_g.reshape(1, d), kv_scale, kv_shift, wd,
                             kv_g_kva.reshape(1, KV_LORA_RANK), _pad_lanes(kv_g_kr),
                             wk, wvt, kv_g_kn.reshape(1, QK_NOPE_DIM),
                             cos, s_hi, s_lo, sgn, batch, seq)

    b_w_in_bf, b_w_out_bf = b_w_in.astype(BF16), b_w_out.astype(BF16)
    b_w_uqt_bf = jnp.swapaxes(_query_slabs(b_w_uq), 1, 2).astype(BF16)
    b_g_qa3 = b_g_qa.reshape(N_B_LAYERS, 1, Q_LORA_RANK)
    b_g_qn3 = (b_g_qn * sm_scale).reshape(N_B_LAYERS, QK_NOPE_DIM, 1)
    g1, g2 = _rope_halves(b_g_qr * sm_scale)
    b_g_qr3 = jnp.concatenate([g1, g2, g2, g1], axis=-1).reshape(N_B_LAYERS, LANES, 1)
    q_tab_t = q_tab.T
    for j in range(N_B_LAYERS):
        shift, scale, gate = vecs(mod[N_A_LAYERS + j], 3)
        q_t, z = _mla_in(xf, norm_g[N_A_LAYERS + j].reshape(1, d), scale, shift, b_w_in_bf,
                         b_g_qa3, b_w_uqt_bf, b_g_qn3, b_g_qr3, j, q_tab_t, batch, seq)
        a = _attention(q_t, k_all, vt_all)
        xf = _mla_out(xf, a.reshape(t, ATTN_WIDTH), z, b_w_out_bf, j, gate, seq)

    return xf.reshape(batch, seq, d)
```

```python
import functools
import math

import jax
import jax.numpy as jnp
from jax import lax
from jax.experimental import pallas as pl
from jax.experimental.pallas import tpu as pltpu

F32 = jnp.float32
BF16 = jnp.bfloat16

D_MODEL = 2048
N_A_LAYERS = 2
N_B_LAYERS = 2
DEPTH = N_A_LAYERS + N_B_LAYERS
GM_WIDTH = 2 * D_MODEL
GM_GROUPS = 16
GM_GROUP_DIM = GM_WIDTH // GM_GROUPS
CHUNK = 128
N_HEADS = 16
QK_NOPE_DIM = 128
QK_ROPE_DIM = 64
HALF_ROPE = QK_ROPE_DIM // 2
QK_DIM = QK_NOPE_DIM + QK_ROPE_DIM
V_HEAD_DIM = 128
Q_LORA_RANK = 512
KV_LORA_RANK = 512
ATTN_WIDTH = N_HEADS * V_HEAD_DIM
ROPE_THETA = 10000.0
EPS = 1e-6

LANES = 128
HEAD_SLAB = 2 * LANES
VMEM_LIMIT = 56 * 1024 * 1024

GELU_C = math.sqrt(2.0 / math.pi)


def _silu(a):
    return a / (1.0 + jnp.exp(-a))


def _rms(a, width):
    return a * lax.rsqrt(jnp.sum(a * a, axis=-1, keepdims=True) * (1.0 / width) + EPS)


def _ada_norm(x, g, scale, shift):
    return (_rms(x, x.shape[-1]) * (g * (1.0 + scale)) + shift).astype(BF16)


def _rope(r, cos, sin_hi, sin_lo):
    return (r * cos + pltpu.roll(r, HALF_ROPE, 1) * sin_hi
            + pltpu.roll(r, LANES - HALF_ROPE, 1) * sin_lo)


def _params(n_axes, flags=None):
    return pltpu.CompilerParams(dimension_semantics=("arbitrary",) * n_axes,
                                vmem_limit_bytes=VMEM_LIMIT, flags=flags)


def _resident(shape):
    return pl.BlockSpec(shape, lambda *_: (0,) * len(shape), pipeline_mode=pl.Buffered(1))


def _ada_kernel(c_ref, w_ref, b_ref, o_ref):
    a = _silu(c_ref[...]).astype(BF16)
    o_ref[...] = jnp.dot(a, w_ref[...].astype(BF16), preferred_element_type=F32) + b_ref[...]


def _ada_mod(c_pad, w, b, tn=1024):
    n_layers, d, n = w.shape
    return pl.pallas_call(
        _ada_kernel,
        grid=(n_layers, n // tn),
        in_specs=[pl.BlockSpec(c_pad.shape, lambda l, j: (0, 0)),
                  pl.BlockSpec((None, d, tn), lambda l, j: (l, 0, j)),
                  pl.BlockSpec((None, 1, tn), lambda l, j: (l, 0, j))],
        out_specs=pl.BlockSpec((None, c_pad.shape[0], tn), lambda l, j: (l, 0, j)),
        out_shape=jax.ShapeDtypeStruct((n_layers, c_pad.shape[0], n), F32),
        compiler_params=_params(2),
        name="ada_mod",
    )(c_pad, w, b.reshape(n_layers, 1, n))


def _gmlp_in_kernel(x_ref, g_ref, sc_ref, sh_ref, w_ref, o_ref, h_ref, *, n_gelu_tiles, chunk):
    j = pl.program_id(1)

    @pl.when(j == 0)
    def _():
        h_ref[...] = _ada_norm(x_ref[...], g_ref[...], sc_ref[...], sh_ref[...])

    is_gelu = j < n_gelu_tiles
    c1 = jnp.where(is_gelu, 2.0 * GELU_C, 1.0).astype(F32)
    c3 = jnp.where(is_gelu, 2.0 * GELU_C * 0.044715, 0.0).astype(F32)
    h = h_ref[...]
    for n0 in range(0, o_ref.shape[1], chunk):
        a = jnp.dot(h, w_ref[:, n0:n0 + chunk], preferred_element_type=F32)
        gate = a * (c1 + c3 * (a * a))
        o_ref[:, n0:n0 + chunk] = (a / (1.0 + jnp.exp(-gate))).astype(BF16)


def _layer_block(w, layer, *block):
    n_axes = len(block)
    return pl.BlockSpec((None, *block), lambda *_: (layer,) + (0,) * n_axes,
                        pipeline_mode=pl.Buffered(1))


def _gmlp_in(x, g, scale, shift, w, layer, seq, tm=1024, tn=2048, chunk=512):
    t, d = x.shape
    n = w.shape[2]
    per_batch = seq // tm
    vec = pl.BlockSpec((None, 1, d), lambda i, j: (i // per_batch, 0, 0))
    return pl.pallas_call(
        functools.partial(_gmlp_in_kernel, n_gelu_tiles=2 * GM_WIDTH // tn, chunk=chunk),
        grid=(t // tm, n // tn),
        in_specs=[pl.BlockSpec((tm, d), lambda i, j: (i, 0)),
                  pl.BlockSpec((1, d), lambda i, j: (0, 0)),
                  vec, vec,
                  pl.BlockSpec((None, d, tn), lambda i, j: (layer, 0, j))],
        out_specs=pl.BlockSpec((tm, tn), lambda i, j: (i, j)),
        out_shape=jax.ShapeDtypeStruct((t, n), BF16),
        scratch_shapes=[pltpu.VMEM((tm, d), BF16)],
        compiler_params=_params(2),
        name="gmlp_in",
    )(x, g, scale, shift, w)


def _gmlp_out_kernel(x_ref, u_ref, v_ref, z_ref, lng_ref, lnb_ref, ws_ref, bst_ref, w_ref,
                     gate_ref, o_ref, wsm_ref, vn_ref, y0_ref, y1_ref, *, proj_chunk):
    step = pl.program_id(0)

    @pl.when(step == 0)
    def _():
        row = lax.broadcasted_iota(jnp.int32, (CHUNK, CHUNK), 0)
        col = lax.broadcasted_iota(jnp.int32, (CHUNK, CHUNK), 1)
        for g in range(GM_GROUPS):
            wsm_ref[g] = jnp.where(col <= row, ws_ref[g], 0.0).astype(BF16)
        y1_ref[...] = jnp.zeros(y1_ref.shape, BF16)

    def layer_norm():
        v = v_ref[...].astype(F32)
        mu = jnp.mean(v, axis=-1, keepdims=True)
        vc = v - mu
        var = jnp.mean(vc * vc, axis=-1, keepdims=True)
        vn_ref[...] = (vc * lax.rsqrt(var + EPS) * lng_ref[...] + lnb_ref[...]).astype(BF16)

    def mix(y_ref, c, g):
        rows = pl.ds(c * CHUNK, CHUNK)
        cols = pl.ds(g * GM_GROUP_DIM, GM_GROUP_DIM)
        mixed = jnp.dot(wsm_ref[g], vn_ref[rows, cols], preferred_element_type=F32)
        mixed = mixed + bst_ref[:, g:g + 1]
        y = u_ref[rows, cols].astype(F32) * mixed * z_ref[rows, cols].astype(F32)
        y_ref[rows, cols] = y.astype(BF16)

    def project(y_ref, n0):
        cols = pl.ds(n0, proj_chunk)
        out = jnp.dot(y_ref[...], w_ref[:, cols], preferred_element_type=F32)
        o_ref[:, cols] = x_ref[:, cols] + gate_ref[:, cols] * out

    def both(y_prev_ref, y_next_ref):
        pieces = [layer_norm] + [functools.partial(mix, y_next_ref, c, g)
                                 for c in range(x_ref.shape[0] // CHUNK)
                                 for g in range(GM_GROUPS)]
        n_proj = o_ref.shape[1] // proj_chunk
        per_chunk = -(-len(pieces) // n_proj)
        for i in range(n_proj):
            project(y_prev_ref, i * proj_chunk)
            for piece in pieces[i * per_chunk:(i + 1) * per_chunk]:
                piece()

    @pl.when(step % 2 == 0)
    def _():
        both(y1_ref, y0_ref)

    @pl.when(step % 2 == 1)
    def _():
        both(y0_ref, y1_ref)


def _gmlp_out(x, uvz, ln_g, ln_b, w_s, b_s_t, w_out, layer, gate, seq, tm=256, proj_chunk=256):
    t, d = x.shape
    per_batch = seq // tm
    n_tiles = t // tm
    part = lambda k: pl.BlockSpec((tm, GM_WIDTH), lambda s: (jnp.minimum(s, n_tiles - 1), k))
    prev = lambda s: jnp.maximum(s - 1, 0)
    return pl.pallas_call(
        functools.partial(_gmlp_out_kernel, proj_chunk=proj_chunk),
        grid=(n_tiles + 1,),
        in_specs=[pl.BlockSpec((tm, d), lambda s: (prev(s), 0)),
                  part(0), part(1), part(2),
                  _layer_block(ln_g, layer, 1, GM_WIDTH), _layer_block(ln_b, layer, 1, GM_WIDTH),
                  _layer_block(w_s, layer, *w_s.shape[1:]),
                  _layer_block(b_s_t, layer, *b_s_t.shape[1:]),
                  _layer_block(w_out, layer, *w_out.shape[1:]),
                  pl.BlockSpec((None, 1, d), lambda s: (prev(s) // per_batch, 0, 0))],
        out_specs=pl.BlockSpec((tm, d), lambda s: (prev(s), 0)),
        out_shape=jax.ShapeDtypeStruct((t, d), F32),
        scratch_shapes=[pltpu.VMEM((GM_GROUPS, CHUNK, CHUNK), BF16),
                        pltpu.VMEM((tm, GM_WIDTH), BF16),
                        pltpu.VMEM((tm, GM_WIDTH), BF16),
                        pltpu.VMEM((tm, GM_WIDTH), BF16)],
        compiler_params=_params(1),
        name="gmlp_out",
    )(x, uvz, uvz, uvz, ln_g, ln_b, w_s, b_s_t, w_out, gate)


def _rows_by_cols_t(a, b):
    return lax.dot_general(a, b, (((1,), (1,)), ((), ())), preferred_element_type=F32)


def _kv_kernel(x_ref, g_ref, sc_ref, sh_ref, wd_ref, gkva_ref, gkr_ref, wk_ref, wvt_ref, gkn_ref,
               cos_ref, shi_ref, slo_ref, sgn_ref, k_ref, vt_ref, *, heads_per_chunk):
    h = _ada_norm(x_ref[...], g_ref[...], sc_ref[...], sh_ref[...])
    ckr = jnp.dot(h, wd_ref[...], preferred_element_type=F32)
    c_kv = (_rms(ckr[:, :KV_LORA_RANK], KV_LORA_RANK) * gkva_ref[...]).astype(BF16)
    k_r = _rms(ckr[:, KV_LORA_RANK:], QK_ROPE_DIM) * gkr_ref[...]
    k_r = _rope(k_r, cos_ref[...], shi_ref[...], slo_ref[...])
    k_x = (k_r + pltpu.roll(k_r, QK_ROPE_DIM, 1) * sgn_ref[...]).astype(BF16)
    for c0 in range(0, N_HEADS, heads_per_chunk):
        heads = slice(c0 * LANES, (c0 + heads_per_chunk) * LANES)
        k_n = jnp.dot(c_kv, wk_ref[:, heads], preferred_element_type=F32)
        v_t = _rows_by_cols_t(wvt_ref[heads, :], c_kv)
        for i in range(heads_per_chunk):
            one = slice(i * LANES, (i + 1) * LANES)
            k_ref[c0 + i, :, :QK_NOPE_DIM] = (_rms(k_n[:, one], QK_NOPE_DIM)
                                              * gkn_ref[...]).astype(BF16)
            k_ref[c0 + i, :, QK_NOPE_DIM:] = k_x
            vt_ref[c0 + i] = v_t[one, :].astype(BF16)


def _kv_proj(x, g, scale, shift, wd, gkva, gkr, wk, wvt, gkn, cos, s_hi, s_lo, sgn, batch, seq,
             tm=512, heads_per_chunk=2):
    t, d = x.shape
    per_batch = seq // tm
    vec = pl.BlockSpec((None, 1, d), lambda i: (i // per_batch, 0, 0))
    tab = pl.BlockSpec((tm, LANES), lambda i: (i % per_batch, 0))
    return pl.pallas_call(
        functools.partial(_kv_kernel, heads_per_chunk=heads_per_chunk),
        grid=(t // tm,),
        in_specs=[pl.BlockSpec((tm, d), lambda i: (i, 0)),
                  _resident((1, d)), vec, vec,
                  _resident(wd.shape), _resident(gkva.shape), _resident(gkr.shape),
                  _resident(wk.shape), _resident(wvt.shape), _resident(gkn.shape),
                  tab, tab, tab, _resident(sgn.shape)],
        out_specs=[pl.BlockSpec((None, N_HEADS, tm, HEAD_SLAB),
                                lambda i: (i // per_batch, 0, i % per_batch, 0)),
                   pl.BlockSpec((None, N_HEADS, V_HEAD_DIM, tm),
                                lambda i: (i // per_batch, 0, 0, i % per_batch))],
        out_shape=[jax.ShapeDtypeStruct((batch, N_HEADS, seq, HEAD_SLAB), BF16),
                   jax.ShapeDtypeStruct((batch, N_HEADS, V_HEAD_DIM, seq), BF16)],
        compiler_params=_params(1),
        name="kv_proj",
    )(x, g, scale, shift, wd, gkva, gkr, wk, wvt, gkn, cos, s_hi, s_lo, sgn)


def _col_rms(a):
    return a * lax.rsqrt(jnp.mean(a * a, axis=0, keepdims=True) + EPS)


def _mla_in_kernel(x_ref, g_ref, sc_ref, sh_ref, win_ref, gqa_ref, wuqt_ref, gqn_ref, gqr_ref,
                   tabt_ref, qt_ref, z_ref, *, chunk, heads_per_chunk):
    h = _ada_norm(x_ref[...], g_ref[...], sc_ref[...], sh_ref[...])
    c_q = jnp.dot(h, win_ref[:, :Q_LORA_RANK], preferred_element_type=F32)
    c_q = (_rms(c_q, Q_LORA_RANK) * gqa_ref[...]).astype(BF16)
    for n0 in range(0, ATTN_WIDTH, chunk):
        z = jnp.dot(h, win_ref[:, Q_LORA_RANK + n0:Q_LORA_RANK + n0 + chunk],
                    preferred_element_type=F32)
        z_ref[:, n0:n0 + chunk] = _silu(z).astype(BF16)
    tm = x_ref.shape[0]
    nope_mult = jnp.broadcast_to(gqn_ref[...], (QK_NOPE_DIM, tm))
    rope_mult = tabt_ref[...] * gqr_ref[...]
    for c0 in range(0, N_HEADS, heads_per_chunk):
        q_t = _rows_by_cols_t(wuqt_ref[c0 * HEAD_SLAB:(c0 + heads_per_chunk) * HEAD_SLAB, :],
                              c_q)
        for i in range(heads_per_chunk):
            q_n = q_t[i * HEAD_SLAB:i * HEAD_SLAB + QK_NOPE_DIM, :]
            q_x = q_t[i * HEAD_SLAB + QK_NOPE_DIM:(i + 1) * HEAD_SLAB, :]
            qt_ref[c0 + i, :QK_NOPE_DIM, :] = (_col_rms(q_n) * nope_mult).astype(BF16)
            qt_ref[c0 + i, QK_NOPE_DIM:, :] = (_col_rms(q_x) * rope_mult).astype(BF16)


def _mla_in(x, g, scale, shift, w_in, gqa, wuqt, gqn, gqr, layer, tabt, batch, seq, tm=512,
            chunk=512, heads_per_chunk=2):
    t, d = x.shape
    per_batch = seq // tm
    vec = pl.BlockSpec((None, 1, d), lambda i: (i // per_batch, 0, 0))
    return pl.pallas_call(
        functools.partial(_mla_in_kernel, chunk=chunk, heads_per_chunk=heads_per_chunk),
        grid=(t // tm,),
        in_specs=[pl.BlockSpec((tm, d), lambda i: (i, 0)),
                  _resident((1, d)), vec, vec,
                  _layer_block(w_in, layer, *w_in.shape[1:]),
                  _layer_block(gqa, layer, *gqa.shape[1:]),
                  _layer_block(wuqt, layer, *wuqt.shape[1:]),
                  _layer_block(gqn, layer, *gqn.shape[1:]),
                  _layer_block(gqr, layer, *gqr.shape[1:]),
                  pl.BlockSpec((LANES, tm), lambda i: (0, i % per_batch))],
        out_specs=[pl.BlockSpec((None, N_HEADS, HEAD_SLAB, tm),
                                lambda i: (i // per_batch, 0, 0, i % per_batch)),
                   pl.BlockSpec((tm, ATTN_WIDTH), lambda i: (i, 0))],
        out_shape=[jax.ShapeDtypeStruct((batch, N_HEADS, HEAD_SLAB, seq), BF16),
                   jax.ShapeDtypeStruct((t, ATTN_WIDTH), BF16)],
        compiler_params=_params(1),
        name="mla_in",
    )(x, g, scale, shift, w_in, gqa, wuqt, gqn, gqr, tabt)


def _attn_kernel(qt_ref, k_ref, vt_ref, o_ref, sa_ref, sb_ref, ma_ref, mb_ref, m_ref, l_ref,
                 acc_ref, *, tile, band):
    qi = pl.program_id(2)

    def causal(s):
        key = lax.broadcasted_iota(jnp.int32, s.shape, 0)
        query = lax.broadcasted_iota(jnp.int32, s.shape, 1)
        return jnp.where(key <= query, s, -jnp.inf)

    def scores(j, s_ref, tile_max_ref):
        k = k_ref[pl.ds(pl.multiple_of(j * tile, tile), tile), :]
        s = jnp.dot(k, qt_ref[...], preferred_element_type=F32)
        s_ref[...] = s
        tile_max_ref[...] = jnp.max(s, axis=0, keepdims=True)

    def update(j, s_ref, tile_max_ref):
        vt = vt_ref[:, pl.ds(pl.multiple_of(j * tile, tile), tile)]
        m = m_ref[...]
        m_new = jnp.maximum(m, tile_max_ref[...])
        alpha = jnp.exp2(m - m_new)
        p = jnp.exp2(s_ref[...] - m_new)
        m_ref[...] = m_new
        l_ref[...] = alpha * l_ref[...] + jnp.sum(p, axis=0, keepdims=True)
        acc_ref[...] = alpha * acc_ref[...] + jnp.dot(vt, p.astype(BF16),
                                                      preferred_element_type=F32)

    def scores_diagonal(j, s_ref, tile_max_ref):
        start = pl.multiple_of(j * tile, tile)
        for lo in range(0, tile, band):
            s = causal(jnp.dot(k_ref[pl.ds(start + lo, band), :], qt_ref[:, lo:],
                               preferred_element_type=F32))
            s_ref[lo:lo + band, lo:] = s
            band_max = jnp.max(s, axis=0, keepdims=True)
            if lo == 0:
                tile_max_ref[...] = band_max
            else:
                tile_max_ref[:, lo:] = jnp.maximum(tile_max_ref[:, lo:], band_max)

    def update_diagonal(j, s_ref, tile_max_ref):
        start = pl.multiple_of(j * tile, tile)
        m = m_ref[...]
        m_new = jnp.maximum(m, tile_max_ref[...])
        alpha = jnp.exp2(m - m_new)
        m_ref[...] = m_new
        l_ref[...] = alpha * l_ref[...]
        acc_ref[...] = alpha * acc_ref[...]
        for lo in range(0, tile, band):
            p = jnp.exp2(s_ref[lo:lo + band, lo:] - m_new[:, lo:])
            l_ref[:, lo:] += jnp.sum(p, axis=0, keepdims=True)
            acc_ref[:, lo:] += jnp.dot(vt_ref[:, pl.ds(start + lo, band)], p.astype(BF16),
                                       preferred_element_type=F32)

    def finish():
        o_ref[...] = (acc_ref[...] / l_ref[...]).T.astype(o_ref.dtype)

    m_ref[...] = jnp.full(m_ref.shape, -jnp.inf, F32)
    l_ref[...] = jnp.zeros(l_ref.shape, F32)
    acc_ref[...] = jnp.zeros(acc_ref.shape, F32)

    @pl.when(qi > 0)
    def _():
        scores(0, sa_ref, ma_ref)

    @pl.loop(0, jnp.maximum(qi - 1, 0) // 2)
    def _(jj):
        scores(2 * jj + 1, sb_ref, mb_ref)
        update(2 * jj, sa_ref, ma_ref)
        scores(2 * jj + 2, sa_ref, ma_ref)
        update(2 * jj + 1, sb_ref, mb_ref)

    @pl.when(jnp.logical_and(qi > 0, qi % 2 == 0))
    def _():
        scores(qi - 1, sb_ref, mb_ref)
        update(qi - 2, sa_ref, ma_ref)
        scores_diagonal(qi, sa_ref, ma_ref)
        update(qi - 1, sb_ref, mb_ref)
        update_diagonal(qi, sa_ref, ma_ref)
        finish()

    @pl.when(qi % 2 == 1)
    def _():
        scores_diagonal(qi, sb_ref, mb_ref)
        update(qi - 1, sa_ref, ma_ref)
        update_diagonal(qi, sb_ref, mb_ref)
        finish()

    @pl.when(qi == 0)
    def _():
        scores_diagonal(0, sa_ref, ma_ref)
        update_diagonal(0, sa_ref, ma_ref)
        finish()


def _attention(qt, k, vt, tile=1024, band=256):
    batch, heads, width, seq = qt.shape
    return pl.pallas_call(
        functools.partial(_attn_kernel, tile=tile, band=band),
        grid=(batch, heads, seq // tile),
        in_specs=[pl.BlockSpec((None, None, width, tile), lambda b, h, i: (b, h, 0, i)),
                  pl.BlockSpec((None, None, seq, width), lambda b, h, i: (b, h, 0, 0)),
                  pl.BlockSpec((None, None, V_HEAD_DIM, seq), lambda b, h, i: (b, h, 0, 0))],
        out_specs=pl.BlockSpec((None, tile, V_HEAD_DIM), lambda b, h, i: (b, i, h)),
        out_shape=jax.ShapeDtypeStruct((batch, seq, heads * V_HEAD_DIM), BF16),
        scratch_shapes=[pltpu.VMEM((tile, tile), F32), pltpu.VMEM((tile, tile), F32),
                        pltpu.VMEM((1, tile), F32), pltpu.VMEM((1, tile), F32),
                        pltpu.VMEM((1, tile), F32), pltpu.VMEM((1, tile), F32),
                        pltpu.VMEM((V_HEAD_DIM, tile), F32)],
        compiler_params=_params(3),
        name="attention",
    )(qt, k, vt)


def _mla_out_kernel(x_ref, a_ref, z_ref, w_ref, gate_ref, o_ref):
    y = (a_ref[...].astype(F32) * z_ref[...].astype(F32)).astype(BF16)
    out = jnp.dot(y, w_ref[...], preferred_element_type=F32)
    o_ref[...] = x_ref[...] + gate_ref[...] * out


def _mla_out(x, a, z, w_out, layer, gate, seq, tm=512):
    t, d = x.shape
    per_batch = seq // tm
    row = lambda w: pl.BlockSpec((tm, w), lambda i: (i, 0))
    return pl.pallas_call(
        _mla_out_kernel,
        grid=(t // tm,),
        in_specs=[row(d), row(ATTN_WIDTH), row(ATTN_WIDTH),
                  _layer_block(w_out, layer, *w_out.shape[1:]),
                  pl.BlockSpec((None, 1, d), lambda i: (i // per_batch, 0, 0))],
        out_specs=row(d),
        out_shape=jax.ShapeDtypeStruct((t, d), F32),
        compiler_params=_params(1),
        name="mla_out",
    )(x, a, z, w_out, gate)


def _rope_halves(r):
    return r[..., :HALF_ROPE], r[..., HALF_ROPE:]


def _query_slabs(w):
    lead = w.shape[:-1]
    w = w.reshape(*lead, N_HEADS, QK_DIM)
    x1, x2 = _rope_halves(w[..., QK_NOPE_DIM:])
    w = jnp.concatenate([w[..., :QK_NOPE_DIM], x1, x2, x2, x1], axis=-1)
    return w.reshape(*lead, N_HEADS * HEAD_SLAB)


def _rope_tables(seq):
    pos = jnp.arange(seq, dtype=F32)
    inv_freq = ROPE_THETA ** (-jnp.arange(0, QK_ROPE_DIM, 2, dtype=F32) / QK_ROPE_DIM)
    ang = pos[:, None] * inv_freq[None, :]
    cos, sin = jnp.cos(ang), jnp.sin(ang)
    zero = jnp.zeros_like(cos)
    pad = jnp.zeros((seq, LANES - QK_ROPE_DIM), F32)
    key_tables = (jnp.concatenate([cos, cos, pad], axis=1),
                  jnp.concatenate([zero, sin, pad], axis=1),
                  jnp.concatenate([-sin, zero, pad], axis=1))
    query_table = jnp.concatenate([cos, cos, sin, sin], axis=1)
    return key_tables, query_table


def _pad_lanes(g):
    return jnp.pad(g, (0, LANES - g.shape[0])).reshape(1, LANES)


def kernel(x, c, ada_w, ada_b, norm_g, a_w_in, a_ln_g, a_ln_b, a_w_s, a_b_s, a_w_out, kv_ada_w, kv_ada_b, kv_norm_g, kv_w_dkv, kv_g_kva, kv_w_ukv, kv_g_kn, kv_g_kr, b_w_in, b_g_qa, b_w_uq, b_g_qn, b_g_qr, b_w_out):
    batch, seq, d = x.shape
    t = batch * seq
    sm_scale = QK_DIM ** -0.5 * math.log2(math.e)

    c_pad = jnp.pad(c, ((0, 8 - batch), (0, 0)))
    mod = _ada_mod(c_pad, ada_w, ada_b)[:, :batch]
    kv_mod = _ada_mod(c_pad, kv_ada_w[None], kv_ada_b[None])[0, :batch]
    vecs = lambda m, n: [v.reshape(batch, 1, d) for v in jnp.split(m, n, axis=-1)]

    (cos, s_hi, s_lo), q_tab = _rope_tables(seq)
    sgn = jnp.concatenate([jnp.zeros((QK_ROPE_DIM,), F32), -jnp.ones((HALF_ROPE,), F32),
                           jnp.ones((HALF_ROPE,), F32)]).reshape(1, LANES)
    xf = x.reshape(t, d)

    a_w_in_bf, a_w_out_bf = a_w_in.astype(BF16), a_w_out.astype(BF16)
    a_ln_g3 = a_ln_g.reshape(N_A_LAYERS, 1, GM_WIDTH)
    a_ln_b3 = a_ln_b.reshape(N_A_LAYERS, 1, GM_WIDTH)
    a_b_s_t = jnp.swapaxes(a_b_s, 1, 2)
    for i in range(N_A_LAYERS):
        shift, scale, gate = vecs(mod[i], 3)
        uvz = _gmlp_in(xf, norm_g[i].reshape(1, d), scale, shift, a_w_in_bf, i, seq)
        xf = _gmlp_out(xf, uvz, a_ln_g3, a_ln_b3, a_w_s, a_b_s_t, a_w_out_bf, i, gate, seq)

    kv_shift, kv_scale = vecs(kv_mod, 2)
    wd = jnp.pad(kv_w_dkv, ((0, 0), (0, LANES - QK_ROPE_DIM))).astype(BF16)
    w_ukv = kv_w_ukv.reshape(KV_LORA_RANK, N_HEADS, QK_NOPE_DIM + V_HEAD_DIM).astype(BF16)
    wk = w_ukv[..., :QK_NOPE_DIM].reshape(KV_LORA_RANK, N_HEADS * QK_NOPE_DIM)
    wvt = w_ukv[..., QK_NOPE_DIM:].reshape(KV_LORA_RANK, N_HEADS * V_HEAD_DIM).T
    k_all, vt_all = _kv_proj(xf, kv_norm_g.reshape(1, d), kv_scale, kv_shift, wd,
                             kv_g_kva.reshape(1, KV_LORA_RANK), _pad_lanes(kv_g_kr),
                             wk, wvt, kv_g_kn.reshape(1, QK_NOPE_DIM),
                             cos, s_hi, s_lo, sgn, batch, seq)

    b_w_in_bf, b_w_out_bf = b_w_in.astype(BF16), b_w_out.astype(BF16)
    b_w_uqt_bf = jnp.swapaxes(_query_slabs(b_w_uq), 1, 2).astype(BF16)
    b_g_qa3 = b_g_qa.reshape(N_B_LAYERS, 1, Q_LORA_RANK)
    b_g_qn3 = (b_g_qn * sm_scale).reshape(N_B_LAYERS, QK_NOPE_DIM, 1)
    g1, g2 = _rope_halves(b_g_qr * sm_scale)
    b_g_qr3 = jnp.concatenate([g1, g2, g2, g1], axis=-1).reshape(N_B_LAYERS, LANES, 1)
    q_tab_t = q_tab.T
    for j in range(N_B_LAYERS):
        shift, scale, gate = vecs(mod[N_A_LAYERS + j], 3)
        q_t, z = _mla_in(xf, norm_g[N_A_LAYERS + j].reshape(1, d), scale, shift, b_w_in_bf,
                         b_g_qa3, b_w_uqt_bf, b_g_qn3, b_g_qr3, j, q_tab_t, batch, seq)
        a = _attention(q_t, k_all, vt_all)
        xf = _mla_out(xf, a.reshape(t, ATTN_WIDTH), z, b_w_out_bf, j, gate, seq)

    return xf.reshape(batch, seq, d)
```

```python
import functools
import math

import jax
import jax.numpy as jnp
from jax import lax
from jax.experimental import pallas as pl
from jax.experimental.pallas import tpu as pltpu

F32 = jnp.float32
BF16 = jnp.bfloat16

D_MODEL = 2048
N_A_LAYERS = 2
N_B_LAYERS = 2
DEPTH = N_A_LAYERS + N_B_LAYERS
GM_WIDTH = 2 * D_MODEL
GM_GROUPS = 16
GM_GROUP_DIM = GM_WIDTH // GM_GROUPS
CHUNK = 128
N_HEADS = 16
QK_NOPE_DIM = 128
QK_ROPE_DIM = 64
HALF_ROPE = QK_ROPE_DIM // 2
QK_DIM = QK_NOPE_DIM + QK_ROPE_DIM
V_HEAD_DIM = 128
Q_LORA_RANK = 512
KV_LORA_RANK = 512
ATTN_WIDTH = N_HEADS * V_HEAD_DIM
ROPE_THETA = 10000.0
EPS = 1e-6

LANES = 128
HEAD_SLAB = 2 * LANES
VMEM_LIMIT = 56 * 1024 * 1024

GELU_C = math.sqrt(2.0 / math.pi)


def _silu(a):
    half = 0.5 * a
    return half * (1.0 + jnp.tanh(half))


def _rms(a, width):
    return a * lax.rsqrt(jnp.sum(a * a, axis=-1, keepdims=True) * (1.0 / width) + EPS)


def _ada_norm(x, g, scale, shift):
    return (_rms(x, x.shape[-1]) * (g * (1.0 + scale)) + shift).astype(BF16)


def _rope(r, cos, sin_hi, sin_lo):
    return (r * cos + pltpu.roll(r, HALF_ROPE, 1) * sin_hi
            + pltpu.roll(r, LANES - HALF_ROPE, 1) * sin_lo)


def _params(n_axes, flags=None):
    return pltpu.CompilerParams(dimension_semantics=("arbitrary",) * n_axes,
                                vmem_limit_bytes=VMEM_LIMIT, flags=flags)


def _resident(shape):
    return pl.BlockSpec(shape, lambda *_: (0,) * len(shape), pipeline_mode=pl.Buffered(1))


def _ada_kernel(c_ref, w_ref, b_ref, o_ref):
    a = _silu(c_ref[...]).astype(BF16)
    o_ref[...] = jnp.dot(a, w_ref[...].astype(BF16), preferred_element_type=F32) + b_ref[...]


def _ada_mod(c_pad, w, b, tn=1024):
    n_layers, d, n = w.shape
    return pl.pallas_call(
        _ada_kernel,
        grid=(n_layers, n // tn),
        in_specs=[pl.BlockSpec(c_pad.shape, lambda l, j: (0, 0)),
                  pl.BlockSpec((None, d, tn), lambda l, j: (l, 0, j)),
                  pl.BlockSpec((None, 1, tn), lambda l, j: (l, 0, j))],
        out_specs=pl.BlockSpec((None, c_pad.shape[0], tn), lambda l, j: (l, 0, j)),
        out_shape=jax.ShapeDtypeStruct((n_layers, c_pad.shape[0], n), F32),
        compiler_params=_params(2),
        name="ada_mod",
    )(c_pad, w, b.reshape(n_layers, 1, n))


def _gmlp_in_kernel(x_ref, g_ref, sc_ref, sh_ref, w_ref, o_ref, h_ref, *, n_gelu_tiles, chunk):
    j = pl.program_id(1)

    @pl.when(j == 0)
    def _():
        h_ref[...] = _ada_norm(x_ref[...], g_ref[...], sc_ref[...], sh_ref[...])

    is_gelu = j < n_gelu_tiles
    c1 = jnp.where(is_gelu, GELU_C, 0.5).astype(F32)
    c3 = jnp.where(is_gelu, GELU_C * 0.044715, 0.0).astype(F32)
    h = h_ref[...]
    for n0 in range(0, o_ref.shape[1], chunk):
        a = jnp.dot(h, w_ref[:, n0:n0 + chunk], preferred_element_type=F32)
        gate = jnp.tanh(a * (c1 + c3 * (a * a)))
        o_ref[:, n0:n0 + chunk] = ((0.5 * a) * (1.0 + gate)).astype(BF16)


def _layer_block(w, layer, *block):
    n_axes = len(block)
    return pl.BlockSpec((None, *block), lambda *_: (layer,) + (0,) * n_axes,
                        pipeline_mode=pl.Buffered(1))


def _gmlp_in(x, g, scale, shift, w, layer, seq, tm=1024, tn=2048, chunk=512):
    t, d = x.shape
    n = w.shape[2]
    per_batch = seq // tm
    vec = pl.BlockSpec((None, 1, d), lambda i, j: (i // per_batch, 0, 0))
    return pl.pallas_call(
        functools.partial(_gmlp_in_kernel, n_gelu_tiles=2 * GM_WIDTH // tn, chunk=chunk),
        grid=(t // tm, n // tn),
        in_specs=[pl.BlockSpec((tm, d), lambda i, j: (i, 0)),
                  pl.BlockSpec((1, d), lambda i, j: (0, 0)),
                  vec, vec,
                  pl.BlockSpec((None, d, tn), lambda i, j: (layer, 0, j))],
        out_specs=pl.BlockSpec((tm, tn), lambda i, j: (i, j)),
        out_shape=jax.ShapeDtypeStruct((t, n), BF16),
        scratch_shapes=[pltpu.VMEM((tm, d), BF16)],
        compiler_params=_params(2),
        name="gmlp_in",
    )(x, g, scale, shift, w)


def _gmlp_out_kernel(x_ref, u_ref, v_ref, z_ref, lng_ref, lnb_ref, ws_ref, bst_ref, w_ref,
                     gate_ref, o_ref, wsm_ref, vn_ref, y0_ref, y1_ref, *, proj_chunk):
    step = pl.program_id(0)

    @pl.when(step == 0)
    def _():
        row = lax.broadcasted_iota(jnp.int32, (CHUNK, CHUNK), 0)
        col = lax.broadcasted_iota(jnp.int32, (CHUNK, CHUNK), 1)
        for g in range(GM_GROUPS):
            wsm_ref[g] = jnp.where(col <= row, ws_ref[g], 0.0).astype(BF16)
        y1_ref[...] = jnp.zeros(y1_ref.shape, BF16)

    def layer_norm():
        v = v_ref[...].astype(F32)
        mu = jnp.mean(v, axis=-1, keepdims=True)
        vc = v - mu
        var = jnp.mean(vc * vc, axis=-1, keepdims=True)
        vn_ref[...] = (vc * lax.rsqrt(var + EPS) * lng_ref[...] + lnb_ref[...]).astype(BF16)

    def mix(y_ref, c, g):
        rows = pl.ds(c * CHUNK, CHUNK)
        cols = pl.ds(g * GM_GROUP_DIM, GM_GROUP_DIM)
        mixed = jnp.dot(wsm_ref[g], vn_ref[rows, cols], preferred_element_type=F32)
        mixed = mixed + bst_ref[:, g:g + 1]
        y = u_ref[rows, cols].astype(F32) * mixed * z_ref[rows, cols].astype(F32)
        y_ref[rows, cols] = y.astype(BF16)

    def project(y_ref, n0):
        cols = pl.ds(n0, proj_chunk)
        out = jnp.dot(y_ref[...], w_ref[:, cols], preferred_element_type=F32)
        o_ref[:, cols] = x_ref[:, cols] + gate_ref[:, cols] * out

    def both(y_prev_ref, y_next_ref):
        pieces = [layer_norm] + [functools.partial(mix, y_next_ref, c, g)
                                 for c in range(x_ref.shape[0] // CHUNK)
                                 for g in range(GM_GROUPS)]
        n_proj = o_ref.shape[1] // proj_chunk
        per_chunk = -(-len(pieces) // n_proj)
        for i in range(n_proj):
            project(y_prev_ref, i * proj_chunk)
            for piece in pieces[i * per_chunk:(i + 1) * per_chunk]:
                piece()

    @pl.when(step % 2 == 0)
    def _():
        both(y1_ref, y0_ref)

    @pl.when(step % 2 == 1)
    def _():
        both(y0_ref, y1_ref)


def _gmlp_out(x, uvz, ln_g, ln_b, w_s, b_s_t, w_out, layer, gate, seq, tm=256, proj_chunk=256):
    t, d = x.shape
    per_batch = seq // tm
    n_tiles = t // tm
    part = lambda k: pl.BlockSpec((tm, GM_WIDTH), lambda s: (jnp.minimum(s, n_tiles - 1), k))
    prev = lambda s: jnp.maximum(s - 1, 0)
    return pl.pallas_call(
        functools.partial(_gmlp_out_kernel, proj_chunk=proj_chunk),
        grid=(n_tiles + 1,),
        in_specs=[pl.BlockSpec((tm, d), lambda s: (prev(s), 0)),
                  part(0), part(1), part(2),
                  _layer_block(ln_g, layer, 1, GM_WIDTH), _layer_block(ln_b, layer, 1, GM_WIDTH),
                  _layer_block(w_s, layer, *w_s.shape[1:]),
                  _layer_block(b_s_t, layer, *b_s_t.shape[1:]),
                  _layer_block(w_out, layer, *w_out.shape[1:]),
                  pl.BlockSpec((None, 1, d), lambda s: (prev(s) // per_batch, 0, 0))],
        out_specs=pl.BlockSpec((tm, d), lambda s: (prev(s), 0)),
        out_shape=jax.ShapeDtypeStruct((t, d), F32),
        scratch_shapes=[pltpu.VMEM((GM_GROUPS, CHUNK, CHUNK), BF16),
                        pltpu.VMEM((tm, GM_WIDTH), BF16),
                        pltpu.VMEM((tm, GM_WIDTH), BF16),
                        pltpu.VMEM((tm, GM_WIDTH), BF16)],
        compiler_params=_params(1),
        name="gmlp_out",
    )(x, uvz, uvz, uvz, ln_g, ln_b, w_s, b_s_t, w_out, gate)


def _rows_by_cols_t(a, b):
    return lax.dot_general(a, b, (((1,), (1,)), ((), ())), preferred_element_type=F32)


def _kv_kernel(x_ref, g_ref, sc_ref, sh_ref, wd_ref, gkva_ref, gkr_ref, wk_ref, wvt_ref, gkn_ref,
               cos_ref, shi_ref, slo_ref, sgn_ref, k_ref, vt_ref, *, heads_per_chunk):
    h = _ada_norm(x_ref[...], g_ref[...], sc_ref[...], sh_ref[...])
    ckr = jnp.dot(h, wd_ref[...], preferred_element_type=F32)
    c_kv = (_rms(ckr[:, :KV_LORA_RANK], KV_LORA_RANK) * gkva_ref[...]).astype(BF16)
    k_r = _rms(ckr[:, KV_LORA_RANK:], QK_ROPE_DIM) * gkr_ref[...]
    k_r = _rope(k_r, cos_ref[...], shi_ref[...], slo_ref[...])
    k_x = (k_r + pltpu.roll(k_r, QK_ROPE_DIM, 1) * sgn_ref[...]).astype(BF16)
    for c0 in range(0, N_HEADS, heads_per_chunk):
        heads = slice(c0 * LANES, (c0 + heads_per_chunk) * LANES)
        k_n = jnp.dot(c_kv, wk_ref[:, heads], preferred_element_type=F32)
        v_t = _rows_by_cols_t(wvt_ref[heads, :], c_kv)
        for i in range(heads_per_chunk):
            one = slice(i * LANES, (i + 1) * LANES)
            k_ref[c0 + i, :, :QK_NOPE_DIM] = (_rms(k_n[:, one], QK_NOPE_DIM)
                                              * gkn_ref[...]).astype(BF16)
            k_ref[c0 + i, :, QK_NOPE_DIM:] = k_x
            vt_ref[c0 + i] = v_t[one, :].astype(BF16)


def _kv_proj(x, g, scale, shift, wd, gkva, gkr, wk, wvt, gkn, cos, s_hi, s_lo, sgn, batch, seq,
             tm=512, heads_per_chunk=2):
    t, d = x.shape
    per_batch = seq // tm
    vec = pl.BlockSpec((None, 1, d), lambda i: (i // per_batch, 0, 0))
    tab = pl.BlockSpec((tm, LANES), lambda i: (i % per_batch, 0))
    return pl.pallas_call(
        functools.partial(_kv_kernel, heads_per_chunk=heads_per_chunk),
        grid=(t // tm,),
        in_specs=[pl.BlockSpec((tm, d), lambda i: (i, 0)),
                  _resident((1, d)), vec, vec,
                  _resident(wd.shape), _resident(gkva.shape), _resident(gkr.shape),
                  _resident(wk.shape), _resident(wvt.shape), _resident(gkn.shape),
                  tab, tab, tab, _resident(sgn.shape)],
        out_specs=[pl.BlockSpec((None, N_HEADS, tm, HEAD_SLAB),
                                lambda i: (i // per_batch, 0, i % per_batch, 0)),
                   pl.BlockSpec((None, N_HEADS, V_HEAD_DIM, tm),
                                lambda i: (i // per_batch, 0, 0, i % per_batch))],
        out_shape=[jax.ShapeDtypeStruct((batch, N_HEADS, seq, HEAD_SLAB), BF16),
                   jax.ShapeDtypeStruct((batch, N_HEADS, V_HEAD_DIM, seq), BF16)],
        compiler_params=_params(1),
        name="kv_proj",
    )(x, g, scale, shift, wd, gkva, gkr, wk, wvt, gkn, cos, s_hi, s_lo, sgn)


def _col_rms(a):
    return a * lax.rsqrt(jnp.mean(a * a, axis=0, keepdims=True) + EPS)


def _mla_in_kernel(x_ref, g_ref, sc_ref, sh_ref, win_ref, gqa_ref, wuqt_ref, gqn_ref, gqr_ref,
                   tabt_ref, qt_ref, z_ref, *, chunk, heads_per_chunk):
    h = _ada_norm(x_ref[...], g_ref[...], sc_ref[...], sh_ref[...])
    c_q = jnp.dot(h, win_ref[:, :Q_LORA_RANK], preferred_element_type=F32)
    c_q = (_rms(c_q, Q_LORA_RANK) * gqa_ref[...]).astype(BF16)
    for n0 in range(0, ATTN_WIDTH, chunk):
        z = jnp.dot(h, win_ref[:, Q_LORA_RANK + n0:Q_LORA_RANK + n0 + chunk],
                    preferred_element_type=F32)
        z_ref[:, n0:n0 + chunk] = _silu(z).astype(BF16)
    tm = x_ref.shape[0]
    nope_mult = jnp.broadcast_to(gqn_ref[...], (QK_NOPE_DIM, tm))
    rope_mult = tabt_ref[...] * gqr_ref[...]
    for c0 in range(0, N_HEADS, heads_per_chunk):
        q_t = _rows_by_cols_t(wuqt_ref[c0 * HEAD_SLAB:(c0 + heads_per_chunk) * HEAD_SLAB, :],
                              c_q)
        for i in range(heads_per_chunk):
            q_n = q_t[i * HEAD_SLAB:i * HEAD_SLAB + QK_NOPE_DIM, :]
            q_x = q_t[i * HEAD_SLAB + QK_NOPE_DIM:(i + 1) * HEAD_SLAB, :]
            qt_ref[c0 + i, :QK_NOPE_DIM, :] = (_col_rms(q_n) * nope_mult).astype(BF16)
            qt_ref[c0 + i, QK_NOPE_DIM:, :] = (_col_rms(q_x) * rope_mult).astype(BF16)


def _mla_in(x, g, scale, shift, w_in, gqa, wuqt, gqn, gqr, layer, tabt, batch, seq, tm=512,
            chunk=512, heads_per_chunk=2):
    t, d = x.shape
    per_batch = seq // tm
    vec = pl.BlockSpec((None, 1, d), lambda i: (i // per_batch, 0, 0))
    return pl.pallas_call(
        functools.partial(_mla_in_kernel, chunk=chunk, heads_per_chunk=heads_per_chunk),
        grid=(t // tm,),
        in_specs=[pl.BlockSpec((tm, d), lambda i: (i, 0)),
                  _resident((1, d)), vec, vec,
                  _layer_block(w_in, layer, *w_in.shape[1:]),
                  _layer_block(gqa, layer, *gqa.shape[1:]),
                  _layer_block(wuqt, layer, *wuqt.shape[1:]),
                  _layer_block(gqn, layer, *gqn.shape[1:]),
                  _layer_block(gqr, layer, *gqr.shape[1:]),
                  pl.BlockSpec((LANES, tm), lambda i: (0, i % per_batch))],
        out_specs=[pl.BlockSpec((None, N_HEADS, HEAD_SLAB, tm),
                                lambda i: (i // per_batch, 0, 0, i % per_batch)),
                   pl.BlockSpec((tm, ATTN_WIDTH), lambda i: (i, 0))],
        out_shape=[jax.ShapeDtypeStruct((batch, N_HEADS, HEAD_SLAB, seq), BF16),
                   jax.ShapeDtypeStruct((t, ATTN_WIDTH), BF16)],
        compiler_params=_params(1),
        name="mla_in",
    )(x, g, scale, shift, w_in, gqa, wuqt, gqn, gqr, tabt)


def _attn_kernel(qt_ref, k_ref, vt_ref, o_ref, sa_ref, sb_ref, ma_ref, mb_ref, m_ref, l_ref,
                 acc_ref, *, tile):
    qi = pl.program_id(2)
    half = tile // 2

    def causal(s):
        key = lax.broadcasted_iota(jnp.int32, s.shape, 0)
        query = lax.broadcasted_iota(jnp.int32, s.shape, 1)
        return jnp.where(key <= query, s, -jnp.inf)

    def scores(j, s_ref, tile_max_ref):
        k = k_ref[pl.ds(pl.multiple_of(j * tile, tile), tile), :]
        s = jnp.dot(k, qt_ref[...], preferred_element_type=F32)
        s_ref[...] = s
        tile_max_ref[...] = jnp.max(s, axis=0, keepdims=True)

    def update(j, s_ref, tile_max_ref):
        vt = vt_ref[:, pl.ds(pl.multiple_of(j * tile, tile), tile)]
        m = m_ref[...]
        m_new = jnp.maximum(m, tile_max_ref[...])
        alpha = jnp.exp2(m - m_new)
        p = jnp.exp2(s_ref[...] - m_new)
        m_ref[...] = m_new
        l_ref[...] = alpha * l_ref[...] + jnp.sum(p, axis=0, keepdims=True)
        acc_ref[...] = alpha * acc_ref[...] + jnp.dot(vt, p.astype(BF16),
                                                      preferred_element_type=F32)

    def scores_diagonal(j, s_ref, tile_max_ref):
        start = pl.multiple_of(j * tile, tile)
        top = causal(jnp.dot(k_ref[pl.ds(start, half), :], qt_ref[...],
                             preferred_element_type=F32))
        bottom = causal(jnp.dot(k_ref[pl.ds(start + half, half), :], qt_ref[:, half:],
                                preferred_element_type=F32))
        s_ref[:half, :] = top
        s_ref[half:, half:] = bottom
        top_max = jnp.max(top, axis=0, keepdims=True)
        tile_max_ref[:, :half] = top_max[:, :half]
        tile_max_ref[:, half:] = jnp.maximum(top_max[:, half:],
                                             jnp.max(bottom, axis=0, keepdims=True))

    def update_diagonal(j, s_ref, tile_max_ref):
        start = pl.multiple_of(j * tile, tile)
        m = m_ref[...]
        m_new = jnp.maximum(m, tile_max_ref[...])
        alpha = jnp.exp2(m - m_new)
        p_top = jnp.exp2(s_ref[:half, :] - m_new)
        p_bottom = jnp.exp2(s_ref[half:, half:] - m_new[:, half:])
        m_ref[...] = m_new
        top_sum = jnp.sum(p_top, axis=0, keepdims=True)
        pv_top = jnp.dot(vt_ref[:, pl.ds(start, half)], p_top.astype(BF16),
                         preferred_element_type=F32)
        pv_bottom = jnp.dot(vt_ref[:, pl.ds(start + half, half)], p_bottom.astype(BF16),
                            preferred_element_type=F32)
        l_ref[:, :half] = alpha[:, :half] * l_ref[:, :half] + top_sum[:, :half]
        l_ref[:, half:] = (alpha[:, half:] * l_ref[:, half:] + top_sum[:, half:]
                           + jnp.sum(p_bottom, axis=0, keepdims=True))
        acc_ref[:, :half] = alpha[:, :half] * acc_ref[:, :half] + pv_top[:, :half]
        acc_ref[:, half:] = (alpha[:, half:] * acc_ref[:, half:] + pv_top[:, half:]
                             + pv_bottom)

    def finish():
        o_ref[...] = (acc_ref[...] / l_ref[...]).T.astype(o_ref.dtype)

    m_ref[...] = jnp.full(m_ref.shape, -jnp.inf, F32)
    l_ref[...] = jnp.zeros(l_ref.shape, F32)
    acc_ref[...] = jnp.zeros(acc_ref.shape, F32)

    @pl.when(qi > 0)
    def _():
        scores(0, sa_ref, ma_ref)

    @pl.loop(0, jnp.maximum(qi - 1, 0) // 2)
    def _(jj):
        scores(2 * jj + 1, sb_ref, mb_ref)
        update(2 * jj, sa_ref, ma_ref)
        scores(2 * jj + 2, sa_ref, ma_ref)
        update(2 * jj + 1, sb_ref, mb_ref)

    @pl.when(jnp.logical_and(qi > 0, qi % 2 == 0))
    def _():
        scores(qi - 1, sb_ref, mb_ref)
        update(qi - 2, sa_ref, ma_ref)
        scores_diagonal(qi, sa_ref, ma_ref)
        update(qi - 1, sb_ref, mb_ref)
        update_diagonal(qi, sa_ref, ma_ref)
        finish()

    @pl.when(qi % 2 == 1)
    def _():
        scores_diagonal(qi, sb_ref, mb_ref)
        update(qi - 1, sa_ref, ma_ref)
        update_diagonal(qi, sb_ref, mb_ref)
        finish()

    @pl.when(qi == 0)
    def _():
        scores_diagonal(0, sa_ref, ma_ref)
        update_diagonal(0, sa_ref, ma_ref)
        finish()


def _attention(qt, k, vt, tile=1024):
    batch, heads, width, seq = qt.shape
    return pl.pallas_call(
        functools.partial(_attn_kernel, tile=tile),
        grid=(batch, heads, seq // tile),
        in_specs=[pl.BlockSpec((None, None, width, tile), lambda b, h, i: (b, h, 0, i)),
                  pl.BlockSpec((None, None, seq, width), lambda b, h, i: (b, h, 0, 0)),
                  pl.BlockSpec((None, None, V_HEAD_DIM, seq), lambda b, h, i: (b, h, 0, 0))],
        out_specs=pl.BlockSpec((None, tile, V_HEAD_DIM), lambda b, h, i: (b, i, h)),
        out_shape=jax.ShapeDtypeStruct((batch, seq, heads * V_HEAD_DIM), BF16),
        scratch_shapes=[pltpu.VMEM((tile, tile), F32), pltpu.VMEM((tile, tile), F32),
                        pltpu.VMEM((1, tile), F32), pltpu.VMEM((1, tile), F32),
                        pltpu.VMEM((1, tile), F32), pltpu.VMEM((1, tile), F32),
                        pltpu.VMEM((V_HEAD_DIM, tile), F32)],
        compiler_params=_params(3),
        name="attention",
    )(qt, k, vt)


def _mla_out_kernel(x_ref, a_ref, z_ref, w_ref, gate_ref, o_ref):
    y = (a_ref[...].astype(F32) * z_ref[...].astype(F32)).astype(BF16)
    out = jnp.dot(y, w_ref[...], preferred_element_type=F32)
    o_ref[...] = x_ref[...] + gate_ref[...] * out


def _mla_out(x, a, z, w_out, layer, gate, seq, tm=512):
    t, d = x.shape
    per_batch = seq // tm
    row = lambda w: pl.BlockSpec((tm, w), lambda i: (i, 0))
    return pl.pallas_call(
        _mla_out_kernel,
        grid=(t // tm,),
        in_specs=[row(d), row(ATTN_WIDTH), row(ATTN_WIDTH),
                  _layer_block(w_out, layer, *w_out.shape[1:]),
                  pl.BlockSpec((None, 1, d), lambda i: (i // per_batch, 0, 0))],
        out_specs=row(d),
        out_shape=jax.ShapeDtypeStruct((t, d), F32),
        compiler_params=_params(1),
        name="mla_out",
    )(x, a, z, w_out, gate)


def _rope_halves(r):
    return r[..., :HALF_ROPE], r[..., HALF_ROPE:]


def _query_slabs(w):
    lead = w.shape[:-1]
    w = w.reshape(*lead, N_HEADS, QK_DIM)
    x1, x2 = _rope_halves(w[..., QK_NOPE_DIM:])
    w = jnp.concatenate([w[..., :QK_NOPE_DIM], x1, x2, x2, x1], axis=-1)
    return w.reshape(*lead, N_HEADS * HEAD_SLAB)


def _rope_tables(seq):
    pos = jnp.arange(seq, dtype=F32)
    inv_freq = ROPE_THETA ** (-jnp.arange(0, QK_ROPE_DIM, 2, dtype=F32) / QK_ROPE_DIM)
    ang = pos[:, None] * inv_freq[None, :]
    cos, sin = jnp.cos(ang), jnp.sin(ang)
    zero = jnp.zeros_like(cos)
    pad = jnp.zeros((seq, LANES - QK_ROPE_DIM), F32)
    key_tables = (jnp.concatenate([cos, cos, pad], axis=1),
                  jnp.concatenate([zero, sin, pad], axis=1),
                  jnp.concatenate([-sin, zero, pad], axis=1))
    query_table = jnp.concatenate([cos, cos, sin, sin], axis=1)
    return key_tables, query_table


def _pad_lanes(g):
    return jnp.pad(g, (0, LANES - g.shape[0])).reshape(1, LANES)


def kernel(x, c, ada_w, ada_b, norm_g, a_w_in, a_ln_g, a_ln_b, a_w_s, a_b_s, a_w_out, kv_ada_w, kv_ada_b, kv_norm_g, kv_w_dkv, kv_g_kva, kv_w_ukv, kv_g_kn, kv_g_kr, b_w_in, b_g_qa, b_w_uq, b_g_qn, b_g_qr, b_w_out):
    batch, seq, d = x.shape
    t = batch * seq
    sm_scale = QK_DIM ** -0.5 * math.log2(math.e)

    c_pad = jnp.pad(c, ((0, 8 - batch), (0, 0)))
    mod = _ada_mod(c_pad, ada_w, ada_b)[:, :batch]
    kv_mod = _ada_mod(c_pad, kv_ada_w[None], kv_ada_b[None])[0, :batch]
    vecs = lambda m, n: [v.reshape(batch, 1, d) for v in jnp.split(m, n, axis=-1)]

    (cos, s_hi, s_lo), q_tab = _rope_tables(seq)
    sgn = jnp.concatenate([jnp.zeros((QK_ROPE_DIM,), F32), -jnp.ones((HALF_ROPE,), F32),
                           jnp.ones((HALF_ROPE,), F32)]).reshape(1, LANES)
    xf = x.reshape(t, d)

    a_w_in_bf, a_w_out_bf = a_w_in.astype(BF16), a_w_out.astype(BF16)
    a_ln_g3 = a_ln_g.reshape(N_A_LAYERS, 1, GM_WIDTH)
    a_ln_b3 = a_ln_b.reshape(N_A_LAYERS, 1, GM_WIDTH)
    a_b_s_t = jnp.swapaxes(a_b_s, 1, 2)
    for i in range(N_A_LAYERS):
        shift, scale, gate = vecs(mod[i], 3)
        uvz = _gmlp_in(xf, norm_g[i].reshape(1, d), scale, shift, a_w_in_bf, i, seq)
        xf = _gmlp_out(xf, uvz, a_ln_g3, a_ln_b3, a_w_s, a_b_s_t, a_w_out_bf, i, gate, seq)

    kv_shift, kv_scale = vecs(kv_mod, 2)
    wd = jnp.pad(kv_w_dkv, ((0, 0), (0, LANES - QK_ROPE_DIM))).astype(BF16)
    w_ukv = kv_w_ukv.reshape(KV_LORA_RANK, N_HEADS, QK_NOPE_DIM + V_HEAD_DIM).astype(BF16)
    wk = w_ukv[..., :QK_NOPE_DIM].reshape(KV_LORA_RANK, N_HEADS * QK_NOPE_DIM)
    wvt = w_ukv[..., QK_NOPE_DIM:].reshape(KV_LORA_RANK, N_HEADS * V_HEAD_DIM).T
    k_all, vt_all = _kv_proj(xf, kv_norm_g.reshape(1, d), kv_scale, kv_shift, wd,
                             kv_g_kva.reshape(1, KV_LORA_RANK), _pad_lanes(kv_g_kr),
                             wk, wvt, kv_g_kn.reshape(1, QK_NOPE_DIM),
                             cos, s_hi, s_lo, sgn, batch, seq)

    b_w_in_bf, b_w_out_bf = b_w_in.astype(BF16), b_w_out.astype(BF16)
    b_w_uqt_bf = jnp.swapaxes(_query_slabs(b_w_uq), 1, 2).astype(BF16)
    b_g_qa3 = b_g_qa.reshape(N_B_LAYERS, 1, Q_LORA_RANK)
    b_g_qn3 = (b_g_qn * sm_scale).reshape(N_B_LAYERS, QK_NOPE_DIM, 1)
    g1, g2 = _rope_halves(b_g_qr * sm_scale)
    b_g_qr3 = jnp.concatenate([g1, g2, g2, g1], axis=-1).reshape(N_B_LAYERS, LANES, 1)
    q_tab_t = q_tab.T
    for j in range(N_B_LAYERS):
        shift, scale, gate = vecs(mod[N_A_LAYERS + j], 3)
        q_t, z = _mla_in(xf, norm_g[N_A_LAYERS + j].reshape(1, d), scale, shift, b_w_in_bf,
                         b_g_qa3, b_w_uqt_bf, b_g_qn3, b_g_qr3, j, q_tab_t, batch, seq)
        a = _attention(q_t, k_all, vt_all)
        xf = _mla_out(xf, a.reshape(t, ATTN_WIDTH), z, b_w_out_bf, j, gate, seq)

    return xf.reshape(batch, seq, d)
```

```python
import functools
import math

import jax
import jax.numpy as jnp
from jax import lax
from jax.experimental import pallas as pl
from jax.experimental.pallas import tpu as pltpu

F32 = jnp.float32
BF16 = jnp.bfloat16

D_MODEL = 2048
N_A_LAYERS = 2
N_B_LAYERS = 2
GM_WIDTH = 2 * D_MODEL
GM_GROUPS = 16
GM_GROUP_DIM = GM_WIDTH // GM_GROUPS
CHUNK = 128
N_HEADS = 16
QK_NOPE_DIM = 128
QK_ROPE_DIM = 64
HALF_ROPE = QK_ROPE_DIM // 2
QK_DIM = QK_NOPE_DIM + QK_ROPE_DIM
V_HEAD_DIM = 128
Q_LORA_RANK = 512
KV_LORA_RANK = 512
ATTN_WIDTH = N_HEADS * V_HEAD_DIM
ROPE_THETA = 10000.0
EPS = 1e-6

LANES = 128
HEAD_SLAB = 2 * LANES
VMEM_LIMIT = 56 * 1024 * 1024

GELU_C = math.sqrt(2.0 / math.pi)

ADA_TN = 1024
GMLP_IN_TM, GMLP_IN_TN = 1024, 2048
GMLP_IN_CHUNK = 512
GMLP_OUT_TM = 2 * CHUNK
GMLP_OUT_CHUNK = 256
MLA_TM = 512
MLA_IN_CHUNK = 512
HEADS_PER_CHUNK = 2
ATTN_TILE = 1024


def _silu(a):
    half = 0.5 * a
    return half * (1.0 + jnp.tanh(half))


def _rms(a, width):
    return a * lax.rsqrt(jnp.sum(a * a, axis=-1, keepdims=True) * (1.0 / width) + EPS)


def _ada_norm(x, g, scale, shift):
    return (_rms(x, x.shape[-1]) * (g * (1.0 + scale)) + shift).astype(BF16)


def _rope(r, cos, sin_hi, sin_lo):
    return (r * cos + pltpu.roll(r, HALF_ROPE, 1) * sin_hi
            + pltpu.roll(r, LANES - HALF_ROPE, 1) * sin_lo)


def _params(n_axes):
    return pltpu.CompilerParams(dimension_semantics=("arbitrary",) * n_axes,
                                vmem_limit_bytes=VMEM_LIMIT)


def _resident(shape):
    return pl.BlockSpec(shape, lambda *_: (0,) * len(shape), pipeline_mode=pl.Buffered(1))


def _ada_kernel(c_ref, w_ref, b_ref, o_ref):
    a = _silu(c_ref[...]).astype(BF16)
    o_ref[...] = jnp.dot(a, w_ref[...].astype(BF16), preferred_element_type=F32) + b_ref[...]


def _ada_mod(c_pad, w, b, tn=ADA_TN):
    n_layers, d, n = w.shape
    return pl.pallas_call(
        _ada_kernel,
        grid=(n_layers, n // tn),
        in_specs=[pl.BlockSpec(c_pad.shape, lambda l, j: (0, 0)),
                  pl.BlockSpec((None, d, tn), lambda l, j: (l, 0, j)),
                  pl.BlockSpec((None, 1, tn), lambda l, j: (l, 0, j))],
        out_specs=pl.BlockSpec((None, c_pad.shape[0], tn), lambda l, j: (l, 0, j)),
        out_shape=jax.ShapeDtypeStruct((n_layers, c_pad.shape[0], n), F32),
        compiler_params=_params(2),
        name="ada_mod",
    )(c_pad, w, b.reshape(n_layers, 1, n))


def _gmlp_in_kernel(x_ref, g_ref, sc_ref, sh_ref, w_ref, o_ref, h_ref, *, n_gelu_tiles, chunk):
    j = pl.program_id(1)

    @pl.when(j == 0)
    def _():
        h_ref[...] = _ada_norm(x_ref[...], g_ref[...], sc_ref[...], sh_ref[...])

    is_gelu = j < n_gelu_tiles
    c1 = jnp.where(is_gelu, GELU_C, 0.5).astype(F32)
    c3 = jnp.where(is_gelu, GELU_C * 0.044715, 0.0).astype(F32)
    h = h_ref[...]
    for n0 in range(0, o_ref.shape[1], chunk):
        a = jnp.dot(h, w_ref[:, n0:n0 + chunk], preferred_element_type=F32)
        gate = jnp.tanh(a * (c1 + c3 * (a * a)))
        o_ref[:, n0:n0 + chunk] = ((0.5 * a) * (1.0 + gate)).astype(BF16)


def _layer_block(w, layer):
    block = w.shape[1:]
    return pl.BlockSpec((None, *block), lambda *_: (layer,) + (0,) * len(block),
                        pipeline_mode=pl.Buffered(1))


def _gmlp_in(x, g, scale, shift, w, layer, seq, tm=GMLP_IN_TM, tn=GMLP_IN_TN,
             chunk=GMLP_IN_CHUNK):
    t, d = x.shape
    n = w.shape[2]
    per_batch = seq // tm
    vec = pl.BlockSpec((None, 1, d), lambda i, j: (i // per_batch, 0, 0))
    return pl.pallas_call(
        functools.partial(_gmlp_in_kernel, n_gelu_tiles=2 * GM_WIDTH // tn, chunk=chunk),
        grid=(t // tm, n // tn),
        in_specs=[pl.BlockSpec((tm, d), lambda i, j: (i, 0)),
                  pl.BlockSpec((1, d), lambda i, j: (0, 0)),
                  vec, vec,
                  pl.BlockSpec((None, d, tn), lambda i, j: (layer, 0, j))],
        out_specs=pl.BlockSpec((tm, tn), lambda i, j: (i, j)),
        out_shape=jax.ShapeDtypeStruct((t, n), BF16),
        scratch_shapes=[pltpu.VMEM((tm, d), BF16)],
        compiler_params=_params(2),
        name="gmlp_in",
    )(x, g, scale, shift, w)


def _gmlp_out_kernel(x_ref, u_ref, v_ref, z_ref, lng_ref, lnb_ref, ws_ref, bst_ref, w_ref,
                     gate_ref, o_ref, wsm_ref, vn_ref, y0_ref, y1_ref, *, proj_chunk):
    step = pl.program_id(0)

    @pl.when(step == 0)
    def _():
        row = lax.broadcasted_iota(jnp.int32, (CHUNK, CHUNK), 0)
        col = lax.broadcasted_iota(jnp.int32, (CHUNK, CHUNK), 1)
        for g in range(GM_GROUPS):
            wsm_ref[g] = jnp.where(col <= row, ws_ref[g], 0.0).astype(BF16)
        y1_ref[...] = jnp.zeros(y1_ref.shape, BF16)

    def layer_norm():
        v = v_ref[...].astype(F32)
        mu = jnp.mean(v, axis=-1, keepdims=True)
        vc = v - mu
        var = jnp.mean(vc * vc, axis=-1, keepdims=True)
        vn_ref[...] = (vc * lax.rsqrt(var + EPS) * lng_ref[...] + lnb_ref[...]).astype(BF16)

    def mix(y_ref, c, g):
        rows = pl.ds(c * CHUNK, CHUNK)
        cols = pl.ds(g * GM_GROUP_DIM, GM_GROUP_DIM)
        mixed = jnp.dot(wsm_ref[g], vn_ref[rows, cols], preferred_element_type=F32)
        mixed = mixed + bst_ref[:, g:g + 1]
        y = u_ref[rows, cols].astype(F32) * mixed * z_ref[rows, cols].astype(F32)
        y_ref[rows, cols] = y.astype(BF16)

    def project(y_ref, n0):
        cols = pl.ds(n0, proj_chunk)
        out = jnp.dot(y_ref[...], w_ref[:, cols], preferred_element_type=F32)
        o_ref[:, cols] = x_ref[:, cols] + gate_ref[:, cols] * out

    def both(y_prev_ref, y_next_ref):
        pieces = [layer_norm] + [functools.partial(mix, y_next_ref, c, g)
                                 for c in range(x_ref.shape[0] // CHUNK)
                                 for g in range(GM_GROUPS)]
        n_proj = o_ref.shape[1] // proj_chunk
        per_chunk = -(-len(pieces) // n_proj)
        for i in range(n_proj):
            project(y_prev_ref, i * proj_chunk)
            for piece in pieces[i * per_chunk:(i + 1) * per_chunk]:
                piece()

    @pl.when(step % 2 == 0)
    def _():
        both(y1_ref, y0_ref)

    @pl.when(step % 2 == 1)
    def _():
        both(y0_ref, y1_ref)


def _gmlp_out(x, uvz, ln_g, ln_b, w_s, b_s_t, w_out, layer, gate, seq, tm=GMLP_OUT_TM,
              proj_chunk=GMLP_OUT_CHUNK):
    t, d = x.shape
    per_batch = seq // tm
    n_tiles = t // tm
    part = lambda k: pl.BlockSpec((tm, GM_WIDTH), lambda s: (jnp.minimum(s, n_tiles - 1), k))
    prev = lambda s: jnp.maximum(s - 1, 0)
    return pl.pallas_call(
        functools.partial(_gmlp_out_kernel, proj_chunk=proj_chunk),
        grid=(n_tiles + 1,),
        in_specs=[pl.BlockSpec((tm, d), lambda s: (prev(s), 0)),
                  part(0), part(1), part(2),
                  _layer_block(ln_g, layer), _layer_block(ln_b, layer),
                  _layer_block(w_s, layer), _layer_block(b_s_t, layer),
                  _layer_block(w_out, layer),
                  pl.BlockSpec((None, 1, d), lambda s: (prev(s) // per_batch, 0, 0))],
        out_specs=pl.BlockSpec((tm, d), lambda s: (prev(s), 0)),
        out_shape=jax.ShapeDtypeStruct((t, d), F32),
        scratch_shapes=[pltpu.VMEM((GM_GROUPS, CHUNK, CHUNK), BF16),
                        pltpu.VMEM((tm, GM_WIDTH), BF16),
                        pltpu.VMEM((tm, GM_WIDTH), BF16),
                        pltpu.VMEM((tm, GM_WIDTH), BF16)],
        compiler_params=_params(1),
        name="gmlp_out",
    )(x, uvz, uvz, uvz, ln_g, ln_b, w_s, b_s_t, w_out, gate)


def _rows_by_cols_t(a, b):
    return lax.dot_general(a, b, (((1,), (1,)), ((), ())), preferred_element_type=F32)


def _kv_kernel(x_ref, g_ref, sc_ref, sh_ref, wd_ref, gkva_ref, gkr_ref, wk_ref, wvt_ref, gkn_ref,
               cos_ref, shi_ref, slo_ref, sgn_ref, k_ref, vt_ref, *, heads_per_chunk):
    h = _ada_norm(x_ref[...], g_ref[...], sc_ref[...], sh_ref[...])
    ckr = jnp.dot(h, wd_ref[...], preferred_element_type=F32)
    c_kv = (_rms(ckr[:, :KV_LORA_RANK], KV_LORA_RANK) * gkva_ref[...]).astype(BF16)
    k_r = _rms(ckr[:, KV_LORA_RANK:], QK_ROPE_DIM) * gkr_ref[...]
    k_r = _rope(k_r, cos_ref[...], shi_ref[...], slo_ref[...])
    k_x = (k_r + pltpu.roll(k_r, QK_ROPE_DIM, 1) * sgn_ref[...]).astype(BF16)
    for c0 in range(0, N_HEADS, heads_per_chunk):
        heads = slice(c0 * LANES, (c0 + heads_per_chunk) * LANES)
        k_n = jnp.dot(c_kv, wk_ref[:, heads], preferred_element_type=F32)
        v_t = _rows_by_cols_t(wvt_ref[heads, :], c_kv)
        for i in range(heads_per_chunk):
            one = slice(i * LANES, (i + 1) * LANES)
            k_ref[c0 + i, :, :QK_NOPE_DIM] = (_rms(k_n[:, one], QK_NOPE_DIM)
                                              * gkn_ref[...]).astype(BF16)
            k_ref[c0 + i, :, QK_NOPE_DIM:] = k_x
            vt_ref[c0 + i] = v_t[one, :].astype(BF16)


def _kv_proj(x, g, scale, shift, wd, gkva, gkr, wk, wvt, gkn, cos, s_hi, s_lo, sgn, batch, seq,
             tm=MLA_TM, heads_per_chunk=HEADS_PER_CHUNK):
    t, d = x.shape
    per_batch = seq // tm
    vec = pl.BlockSpec((None, 1, d), lambda i: (i // per_batch, 0, 0))
    tab = pl.BlockSpec((tm, LANES), lambda i: (i % per_batch, 0))
    return pl.pallas_call(
        functools.partial(_kv_kernel, heads_per_chunk=heads_per_chunk),
        grid=(t // tm,),
        in_specs=[pl.BlockSpec((tm, d), lambda i: (i, 0)),
                  _resident((1, d)), vec, vec,
                  _resident(wd.shape), _resident(gkva.shape), _resident(gkr.shape),
                  _resident(wk.shape), _resident(wvt.shape), _resident(gkn.shape),
                  tab, tab, tab, _resident(sgn.shape)],
        out_specs=[pl.BlockSpec((None, N_HEADS, tm, HEAD_SLAB),
                                lambda i: (i // per_batch, 0, i % per_batch, 0)),
                   pl.BlockSpec((None, N_HEADS, V_HEAD_DIM, tm),
                                lambda i: (i // per_batch, 0, 0, i % per_batch))],
        out_shape=[jax.ShapeDtypeStruct((batch, N_HEADS, seq, HEAD_SLAB), BF16),
                   jax.ShapeDtypeStruct((batch, N_HEADS, V_HEAD_DIM, seq), BF16)],
        compiler_params=_params(1),
        name="kv_proj",
    )(x, g, scale, shift, wd, gkva, gkr, wk, wvt, gkn, cos, s_hi, s_lo, sgn)


def _col_rms(a):
    return a * lax.rsqrt(jnp.mean(a * a, axis=0, keepdims=True) + EPS)


def _mla_in_kernel(x_ref, g_ref, sc_ref, sh_ref, win_ref, gqa_ref, wuqt_ref, gqn_ref, gqr_ref,
                   tabt_ref, qt_ref, z_ref, *, chunk, heads_per_chunk):
    h = _ada_norm(x_ref[...], g_ref[...], sc_ref[...], sh_ref[...])
    c_q = jnp.dot(h, win_ref[:, :Q_LORA_RANK], preferred_element_type=F32)
    c_q = (_rms(c_q, Q_LORA_RANK) * gqa_ref[...]).astype(BF16)
    for n0 in range(0, ATTN_WIDTH, chunk):
        z = jnp.dot(h, win_ref[:, Q_LORA_RANK + n0:Q_LORA_RANK + n0 + chunk],
                    preferred_element_type=F32)
        z_ref[:, n0:n0 + chunk] = _silu(z).astype(BF16)
    tm = x_ref.shape[0]
    nope_mult = jnp.broadcast_to(gqn_ref[...], (QK_NOPE_DIM, tm))
    rope_mult = tabt_ref[...] * gqr_ref[...]
    for c0 in range(0, N_HEADS, heads_per_chunk):
        q_t = _rows_by_cols_t(wuqt_ref[c0 * HEAD_SLAB:(c0 + heads_per_chunk) * HEAD_SLAB, :],
                              c_q)
        for i in range(heads_per_chunk):
            q_n = q_t[i * HEAD_SLAB:i * HEAD_SLAB + QK_NOPE_DIM, :]
            q_x = q_t[i * HEAD_SLAB + QK_NOPE_DIM:(i + 1) * HEAD_SLAB, :]
            qt_ref[c0 + i, :QK_NOPE_DIM, :] = (_col_rms(q_n) * nope_mult).astype(BF16)
            qt_ref[c0 + i, QK_NOPE_DIM:, :] = (_col_rms(q_x) * rope_mult).astype(BF16)


def _mla_in(x, g, scale, shift, w_in, gqa, wuqt, gqn, gqr, layer, tabt, batch, seq, tm=MLA_TM,
            chunk=MLA_IN_CHUNK, heads_per_chunk=HEADS_PER_CHUNK):
    t, d = x.shape
    per_batch = seq // tm
    vec = pl.BlockSpec((None, 1, d), lambda i: (i // per_batch, 0, 0))
    return pl.pallas_call(
        functools.partial(_mla_in_kernel, chunk=chunk, heads_per_chunk=heads_per_chunk),
        grid=(t // tm,),
        in_specs=[pl.BlockSpec((tm, d), lambda i: (i, 0)),
                  _resident((1, d)), vec, vec,
                  _layer_block(w_in, layer), _layer_block(gqa, layer),
                  _layer_block(wuqt, layer), _layer_block(gqn, layer),
                  _layer_block(gqr, layer),
                  pl.BlockSpec((LANES, tm), lambda i: (0, i % per_batch))],
        out_specs=[pl.BlockSpec((None, N_HEADS, HEAD_SLAB, tm),
                                lambda i: (i // per_batch, 0, 0, i % per_batch)),
                   pl.BlockSpec((tm, ATTN_WIDTH), lambda i: (i, 0))],
        out_shape=[jax.ShapeDtypeStruct((batch, N_HEADS, HEAD_SLAB, seq), BF16),
                   jax.ShapeDtypeStruct((t, ATTN_WIDTH), BF16)],
        compiler_params=_params(1),
        name="mla_in",
    )(x, g, scale, shift, w_in, gqa, wuqt, gqn, gqr, tabt)


def _attn_kernel(qt_ref, k_ref, vt_ref, o_ref, sa_ref, sb_ref, ma_ref, mb_ref, m_ref, l_ref,
                 acc_ref, *, tile):
    qi = pl.program_id(2)
    half = tile // 2

    def causal(s):
        key = lax.broadcasted_iota(jnp.int32, s.shape, 0)
        query = lax.broadcasted_iota(jnp.int32, s.shape, 1)
        return jnp.where(key <= query, s, -jnp.inf)

    def scores(j, s_ref, tile_max_ref):
        k = k_ref[pl.ds(pl.multiple_of(j * tile, tile), tile), :]
        s = jnp.dot(k, qt_ref[...], preferred_element_type=F32)
        s_ref[...] = s
        tile_max_ref[...] = jnp.max(s, axis=0, keepdims=True)

    def update(j, s_ref, tile_max_ref):
        vt = vt_ref[:, pl.ds(pl.multiple_of(j * tile, tile), tile)]
        m = m_ref[...]
        m_new = jnp.maximum(m, tile_max_ref[...])
        alpha = jnp.exp2(m - m_new)
        p = jnp.exp2(s_ref[...] - m_new)
        m_ref[...] = m_new
        l_ref[...] = alpha * l_ref[...] + jnp.sum(p, axis=0, keepdims=True)
        acc_ref[...] = alpha * acc_ref[...] + jnp.dot(vt, p.astype(BF16),
                                                      preferred_element_type=F32)

    def scores_diagonal(j, s_ref, tile_max_ref):
        start = pl.multiple_of(j * tile, tile)
        top = causal(jnp.dot(k_ref[pl.ds(start, half), :], qt_ref[...],
                             preferred_element_type=F32))
        bottom = causal(jnp.dot(k_ref[pl.ds(start + half, half), :], qt_ref[:, half:],
                                preferred_element_type=F32))
        s_ref[:half, :] = top
        s_ref[half:, half:] = bottom
        top_max = jnp.max(top, axis=0, keepdims=True)
        tile_max_ref[:, :half] = top_max[:, :half]
        tile_max_ref[:, half:] = jnp.maximum(top_max[:, half:],
                                             jnp.max(bottom, axis=0, keepdims=True))

    def update_diagonal(j, s_ref, tile_max_ref):
        start = pl.multiple_of(j * tile, tile)
        m = m_ref[...]
        m_new = jnp.maximum(m, tile_max_ref[...])
        alpha = jnp.exp2(m - m_new)
        p_top = jnp.exp2(s_ref[:half, :] - m_new)
        p_bottom = jnp.exp2(s_ref[half:, half:] - m_new[:, half:])
        m_ref[...] = m_new
        top_sum = jnp.sum(p_top, axis=0, keepdims=True)
        pv_top = jnp.dot(vt_ref[:, pl.ds(start, half)], p_top.astype(BF16),
                         preferred_element_type=F32)
        pv_bottom = jnp.dot(vt_ref[:, pl.ds(start + half, half)], p_bottom.astype(BF16),
                            preferred_element_type=F32)
        l_ref[:, :half] = alpha[:, :half] * l_ref[:, :half] + top_sum[:, :half]
        l_ref[:, half:] = (alpha[:, half:] * l_ref[:, half:] + top_sum[:, half:]
                           + jnp.sum(p_bottom, axis=0, keepdims=True))
        acc_ref[:, :half] = alpha[:, :half] * acc_ref[:, :half] + pv_top[:, :half]
        acc_ref[:, half:] = (alpha[:, half:] * acc_ref[:, half:] + pv_top[:, half:]
                             + pv_bottom)

    def finish():
        o_ref[...] = (acc_ref[...] / l_ref[...]).T.astype(o_ref.dtype)

    m_ref[...] = jnp.full(m_ref.shape, -jnp.inf, F32)
    l_ref[...] = jnp.zeros(l_ref.shape, F32)
    acc_ref[...] = jnp.zeros(acc_ref.shape, F32)

    @pl.when(qi > 0)
    def _():
        scores(0, sa_ref, ma_ref)

    @pl.loop(0, jnp.maximum(qi - 1, 0) // 2)
    def _(jj):
        scores(2 * jj + 1, sb_ref, mb_ref)
        update(2 * jj, sa_ref, ma_ref)
        scores(2 * jj + 2, sa_ref, ma_ref)
        update(2 * jj + 1, sb_ref, mb_ref)

    @pl.when(jnp.logical_and(qi > 0, qi % 2 == 0))
    def _():
        scores(qi - 1, sb_ref, mb_ref)
        update(qi - 2, sa_ref, ma_ref)
        scores_diagonal(qi, sa_ref, ma_ref)
        update(qi - 1, sb_ref, mb_ref)
        update_diagonal(qi, sa_ref, ma_ref)
        finish()

    @pl.when(qi % 2 == 1)
    def _():
        scores_diagonal(qi, sb_ref, mb_ref)
        update(qi - 1, sa_ref, ma_ref)
        update_diagonal(qi, sb_ref, mb_ref)
        finish()

    @pl.when(qi == 0)
    def _():
        scores_diagonal(0, sa_ref, ma_ref)
        update_diagonal(0, sa_ref, ma_ref)
        finish()


def _attention(qt, k, vt, tile=ATTN_TILE):
    batch, heads, width, seq = qt.shape
    return pl.pallas_call(
        functools.partial(_attn_kernel, tile=tile),
        grid=(batch, heads, seq // tile),
        in_specs=[pl.BlockSpec((None, None, width, tile), lambda b, h, i: (b, h, 0, i)),
                  pl.BlockSpec((None, None, seq, width), lambda b, h, i: (b, h, 0, 0)),
                  pl.BlockSpec((None, None, V_HEAD_DIM, seq), lambda b, h, i: (b, h, 0, 0))],
        out_specs=pl.BlockSpec((None, tile, V_HEAD_DIM), lambda b, h, i: (b, i, h)),
        out_shape=jax.ShapeDtypeStruct((batch, seq, heads * V_HEAD_DIM), BF16),
        scratch_shapes=[pltpu.VMEM((tile, tile), F32), pltpu.VMEM((tile, tile), F32),
                        pltpu.VMEM((1, tile), F32), pltpu.VMEM((1, tile), F32),
                        pltpu.VMEM((1, tile), F32), pltpu.VMEM((1, tile), F32),
                        pltpu.VMEM((V_HEAD_DIM, tile), F32)],
        compiler_params=_params(3),
        name="attention",
    )(qt, k, vt)


def _mla_out_kernel(x_ref, a_ref, z_ref, w_ref, gate_ref, o_ref):
    y = (a_ref[...].astype(F32) * z_ref[...].astype(F32)).astype(BF16)
    out = jnp.dot(y, w_ref[...], preferred_element_type=F32)
    o_ref[...] = x_ref[...] + gate_ref[...] * out


def _mla_out(x, a, z, w_out, layer, gate, seq, tm=MLA_TM):
    t, d = x.shape
    per_batch = seq // tm
    row = lambda w: pl.BlockSpec((tm, w), lambda i: (i, 0))
    return pl.pallas_call(
        _mla_out_kernel,
        grid=(t // tm,),
        in_specs=[row(d), row(ATTN_WIDTH), row(ATTN_WIDTH),
                  _layer_block(w_out, layer),
                  pl.BlockSpec((None, 1, d), lambda i: (i // per_batch, 0, 0))],
        out_specs=row(d),
        out_shape=jax.ShapeDtypeStruct((t, d), F32),
        compiler_params=_params(1),
        name="mla_out",
    )(x, a, z, w_out, gate)


def _rope_halves(r):
    return r[..., :HALF_ROPE], r[..., HALF_ROPE:]


def _query_slabs(w):
    lead = w.shape[:-1]
    w = w.reshape(*lead, N_HEADS, QK_DIM)
    x1, x2 = _rope_halves(w[..., QK_NOPE_DIM:])
    w = jnp.concatenate([w[..., :QK_NOPE_DIM], x1, x2, x2, x1], axis=-1)
    return w.reshape(*lead, N_HEADS * HEAD_SLAB)


def _rope_tables(seq):
    pos = jnp.arange(seq, dtype=F32)
    inv_freq = ROPE_THETA ** (-jnp.arange(0, QK_ROPE_DIM, 2, dtype=F32) / QK_ROPE_DIM)
    ang = pos[:, None] * inv_freq[None, :]
    cos, sin = jnp.cos(ang), jnp.sin(ang)
    zero = jnp.zeros_like(cos)
    pad = jnp.zeros((seq, LANES - QK_ROPE_DIM), F32)
    key_tables = (jnp.concatenate([cos, cos, pad], axis=1),
                  jnp.concatenate([zero, sin, pad], axis=1),
                  jnp.concatenate([-sin, zero, pad], axis=1))
    query_table = jnp.concatenate([cos, cos, sin, sin], axis=1)
    return key_tables, query_table


def _pad_lanes(g):
    return jnp.pad(g, (0, LANES - g.shape[0])).reshape(1, LANES)


def kernel(x, c, ada_w, ada_b, norm_g, a_w_in, a_ln_g, a_ln_b, a_w_s, a_b_s, a_w_out, kv_ada_w, kv_ada_b, kv_norm_g, kv_w_dkv, kv_g_kva, kv_w_ukv, kv_g_kn, kv_g_kr, b_w_in, b_g_qa, b_w_uq, b_g_qn, b_g_qr, b_w_out):
    batch, seq, d = x.shape
    assert d == D_MODEL and batch <= 8, (batch, d)
    assert seq % max(GMLP_IN_TM, ATTN_TILE, MLA_TM, GMLP_OUT_TM) == 0, seq
    t = batch * seq
    sm_scale = QK_DIM ** -0.5 * math.log2(math.e)

    c_pad = jnp.pad(c, ((0, 8 - batch), (0, 0)))
    mod = _ada_mod(c_pad, ada_w, ada_b)[:, :batch]
    kv_mod = _ada_mod(c_pad, kv_ada_w[None], kv_ada_b[None])[0, :batch]
    vecs = lambda m, n: [v.reshape(batch, 1, d) for v in jnp.split(m, n, axis=-1)]

    (cos, s_hi, s_lo), q_tab = _rope_tables(seq)
    sgn = jnp.concatenate([jnp.zeros((QK_ROPE_DIM,), F32), -jnp.ones((HALF_ROPE,), F32),
                           jnp.ones((HALF_ROPE,), F32)]).reshape(1, LANES)
    xf = x.reshape(t, d)

    a_w_in_bf, a_w_out_bf = a_w_in.astype(BF16), a_w_out.astype(BF16)
    a_ln_g3 = a_ln_g.reshape(N_A_LAYERS, 1, GM_WIDTH)
    a_ln_b3 = a_ln_b.reshape(N_A_LAYERS, 1, GM_WIDTH)
    a_b_s_t = jnp.swapaxes(a_b_s, 1, 2)
    for i in range(N_A_LAYERS):
        shift, scale, gate = vecs(mod[i], 3)
        uvz = _gmlp_in(xf, norm_g[i].reshape(1, d), scale, shift, a_w_in_bf, i, seq)
        xf = _gmlp_out(xf, uvz, a_ln_g3, a_ln_b3, a_w_s, a_b_s_t, a_w_out_bf, i, gate, seq)

    kv_shift, kv_scale = vecs(kv_mod, 2)
    wd = jnp.pad(kv_w_dkv, ((0, 0), (0, LANES - QK_ROPE_DIM))).astype(BF16)
    w_ukv = kv_w_ukv.reshape(KV_LORA_RANK, N_HEADS, QK_NOPE_DIM + V_HEAD_DIM).astype(BF16)
    wk = w_ukv[..., :QK_NOPE_DIM].reshape(KV_LORA_RANK, N_HEADS * QK_NOPE_DIM)
    wvt = w_ukv[..., QK_NOPE_DIM:].reshape(KV_LORA_RANK, N_HEADS * V_HEAD_DIM).T
    k_all, vt_all = _kv_proj(xf, kv_norm_g.reshape(1, d), kv_scale, kv_shift, wd,
                             kv_g_kva.reshape(1, KV_LORA_RANK), _pad_lanes(kv_g_kr),
                             wk, wvt, kv_g_kn.reshape(1, QK_NOPE_DIM),
                             cos, s_hi, s_lo, sgn, batch, seq)

    b_w_in_bf, b_w_out_bf = b_w_in.astype(BF16), b_w_out.astype(BF16)
    b_w_uqt_bf = jnp.swapaxes(_query_slabs(b_w_uq), 1, 2).astype(BF16)
    b_g_qa3 = b_g_qa.reshape(N_B_LAYERS, 1, Q_LORA_RANK)
    b_g_qn3 = (b_g_qn * sm_scale).reshape(N_B_LAYERS, QK_NOPE_DIM, 1)
    g1, g2 = _rope_halves(b_g_qr * sm_scale)
    b_g_qr3 = jnp.concatenate([g1, g2, g2, g1], axis=-1).reshape(N_B_LAYERS, LANES, 1)
    q_tab_t = q_tab.T
    for j in range(N_B_LAYERS):
        shift, scale, gate = vecs(mod[N_A_LAYERS + j], 3)
        q_t, z = _mla_in(xf, norm_g[N_A_LAYERS + j].reshape(1, d), scale, shift, b_w_in_bf,
                         b_g_qa3, b_w_uqt_bf, b_g_qn3, b_g_qr3, j, q_tab_t, batch, seq)
        a = _attention(q_t, k_all, vt_all)
        xf = _mla_out(xf, a.reshape(t, ATTN_WIDTH), z, b_w_out_bf, j, gate, seq)

    return xf.reshape(batch, seq, d)
```

```python
import functools
import math

import jax
import jax.numpy as jnp
from jax import lax
from jax.experimental import pallas as pl
from jax.experimental.pallas import tpu as pltpu

F32 = jnp.float32
BF16 = jnp.bfloat16

D_MODEL = 2048
N_A_LAYERS = 2
N_B_LAYERS = 2
GM_WIDTH = 2 * D_MODEL
GM_GROUPS = 16
GM_GROUP_DIM = GM_WIDTH // GM_GROUPS
CHUNK = 128
N_HEADS = 16
QK_NOPE_DIM = 128
QK_ROPE_DIM = 64
HALF_ROPE = QK_ROPE_DIM // 2
QK_DIM = QK_NOPE_DIM + QK_ROPE_DIM
V_HEAD_DIM = 128
Q_LORA_RANK = 512
KV_LORA_RANK = 512
ATTN_WIDTH = N_HEADS * V_HEAD_DIM
ROPE_THETA = 10000.0
EPS = 1e-6

LANES = 128
HEAD_SLAB = 2 * LANES
VMEM_LIMIT = 56 * 1024 * 1024

GELU_C = math.sqrt(2.0 / math.pi)

ADA_TN = 1024
GMLP_IN_TM, GMLP_IN_TN = 1024, 2048
GMLP_IN_CHUNK = 512
GMLP_OUT_TM = 2 * CHUNK
GMLP_OUT_CHUNK = 256
MLA_TM = 512
MLA_IN_CHUNK = 512
HEADS_PER_CHUNK = 2
ATTN_TILE = 1024


def _silu(a):
    half = 0.5 * a
    return half * (1.0 + jnp.tanh(half))


def _rms(a, width):
    return a * lax.rsqrt(jnp.sum(a * a, axis=-1, keepdims=True) * (1.0 / width) + EPS)


def _ada_norm(x, g, scale, shift):
    return (_rms(x, x.shape[-1]) * (g * (1.0 + scale)) + shift).astype(BF16)


def _rope(r, cos, sin_hi, sin_lo):
    return (r * cos + pltpu.roll(r, HALF_ROPE, 1) * sin_hi
            + pltpu.roll(r, LANES - HALF_ROPE, 1) * sin_lo)


def _params(n_axes):
    return pltpu.CompilerParams(dimension_semantics=("arbitrary",) * n_axes,
                                vmem_limit_bytes=VMEM_LIMIT)


def _resident(shape):
    return pl.BlockSpec(shape, lambda *_: (0,) * len(shape), pipeline_mode=pl.Buffered(1))


def _ada_kernel(c_ref, w_ref, b_ref, o_ref):
    a = _silu(c_ref[...]).astype(BF16)
    o_ref[...] = jnp.dot(a, w_ref[...].astype(BF16), preferred_element_type=F32) + b_ref[...]


def _ada_mod(c_pad, w, b, tn=ADA_TN):
    n_layers, d, n = w.shape
    return pl.pallas_call(
        _ada_kernel,
        grid=(n_layers, n // tn),
        in_specs=[pl.BlockSpec(c_pad.shape, lambda l, j: (0, 0)),
                  pl.BlockSpec((None, d, tn), lambda l, j: (l, 0, j)),
                  pl.BlockSpec((None, 1, tn), lambda l, j: (l, 0, j))],
        out_specs=pl.BlockSpec((None, c_pad.shape[0], tn), lambda l, j: (l, 0, j)),
        out_shape=jax.ShapeDtypeStruct((n_layers, c_pad.shape[0], n), F32),
        compiler_params=_params(2),
        name="ada_mod",
    )(c_pad, w, b.reshape(n_layers, 1, n))


def _gmlp_in_kernel(x_ref, g_ref, sc_ref, sh_ref, w_ref, o_ref, h_ref, *, n_gelu_tiles, chunk):
    j = pl.program_id(1)

    @pl.when(j == 0)
    def _():
        h_ref[...] = _ada_norm(x_ref[...], g_ref[...], sc_ref[...], sh_ref[...])

    is_gelu = j < n_gelu_tiles
    c1 = jnp.where(is_gelu, 2.0 * GELU_C, 1.0).astype(F32)
    c3 = jnp.where(is_gelu, 8.0 * GELU_C * 0.044715, 0.0).astype(F32)
    h = h_ref[...]
    for n0 in range(0, o_ref.shape[1], chunk):
        b = jnp.dot(h, w_ref[:, n0:n0 + chunk], preferred_element_type=F32)
        gate = jnp.tanh(b * (c1 + c3 * (b * b)))
        o_ref[:, n0:n0 + chunk] = (b * (1.0 + gate)).astype(BF16)


def _layer_block(w, layer):
    block = w.shape[1:]
    return pl.BlockSpec((None, *block), lambda *_: (layer,) + (0,) * len(block),
                        pipeline_mode=pl.Buffered(1))


def _gmlp_in(x, g, scale, shift, w, layer, seq, tm=GMLP_IN_TM, tn=GMLP_IN_TN,
             chunk=GMLP_IN_CHUNK):
    t, d = x.shape
    n = w.shape[2]
    per_batch = seq // tm
    vec = pl.BlockSpec((None, 1, d), lambda i, j: (i // per_batch, 0, 0))
    return pl.pallas_call(
        functools.partial(_gmlp_in_kernel, n_gelu_tiles=2 * GM_WIDTH // tn, chunk=chunk),
        grid=(t // tm, n // tn),
        in_specs=[pl.BlockSpec((tm, d), lambda i, j: (i, 0)),
                  pl.BlockSpec((1, d), lambda i, j: (0, 0)),
                  vec, vec,
                  pl.BlockSpec((None, d, tn), lambda i, j: (layer, 0, j))],
        out_specs=pl.BlockSpec((tm, tn), lambda i, j: (i, j)),
        out_shape=jax.ShapeDtypeStruct((t, n), BF16),
        scratch_shapes=[pltpu.VMEM((tm, d), BF16)],
        compiler_params=_params(2),
        name="gmlp_in",
    )(x, g, scale, shift, w)


def _gmlp_out_kernel(x_ref, u_ref, v_ref, z_ref, lng_ref, lnb_ref, ws_ref, bst_ref, w_ref,
                     gate_ref, o_ref, wsm_ref, vn_ref, y0_ref, y1_ref, *, proj_chunk):
    step = pl.program_id(0)

    @pl.when(step == 0)
    def _():
        row = lax.broadcasted_iota(jnp.int32, (CHUNK, CHUNK), 0)
        col = lax.broadcasted_iota(jnp.int32, (CHUNK, CHUNK), 1)
        for g in range(GM_GROUPS):
            wsm_ref[g] = jnp.where(col <= row, ws_ref[g], 0.0).astype(BF16)
        y1_ref[...] = jnp.zeros(y1_ref.shape, BF16)

    def layer_norm():
        v = v_ref[...].astype(F32)
        mu = jnp.mean(v, axis=-1, keepdims=True)
        vc = v - mu
        var = jnp.mean(vc * vc, axis=-1, keepdims=True)
        vn_ref[...] = (vc * lax.rsqrt(var + EPS) * lng_ref[...] + lnb_ref[...]).astype(BF16)

    def mix(y_ref, c, g):
        rows = pl.ds(c * CHUNK, CHUNK)
        cols = pl.ds(g * GM_GROUP_DIM, GM_GROUP_DIM)
        mixed = jnp.dot(wsm_ref[g], vn_ref[rows, cols], preferred_element_type=F32)
        mixed = mixed + bst_ref[:, g:g + 1]
        y = u_ref[rows, cols].astype(F32) * mixed * z_ref[rows, cols].astype(F32)
        y_ref[rows, cols] = y.astype(BF16)

    def project(y_ref, n0):
        cols = pl.ds(n0, proj_chunk)
        out = jnp.dot(y_ref[...], w_ref[:, cols], preferred_element_type=F32)
        o_ref[:, cols] = x_ref[:, cols] + gate_ref[:, cols] * out

    def both(y_prev_ref, y_next_ref):
        pieces = [layer_norm] + [functools.partial(mix, y_next_ref, c, g)
                                 for c in range(x_ref.shape[0] // CHUNK)
                                 for g in range(GM_GROUPS)]
        n_proj = o_ref.shape[1] // proj_chunk
        per_chunk = -(-len(pieces) // n_proj)
        for i in range(n_proj):
            project(y_prev_ref, i * proj_chunk)
            for piece in pieces[i * per_chunk:(i + 1) * per_chunk]:
                piece()

    @pl.when(step % 2 == 0)
    def _():
        both(y1_ref, y0_ref)

    @pl.when(step % 2 == 1)
    def _():
        both(y0_ref, y1_ref)


def _gmlp_out(x, uvz, ln_g, ln_b, w_s, b_s_t, w_out, layer, gate, seq, tm=GMLP_OUT_TM,
              proj_chunk=GMLP_OUT_CHUNK):
    t, d = x.shape
    per_batch = seq // tm
    n_tiles = t // tm
    part = lambda k: pl.BlockSpec((tm, GM_WIDTH), lambda s: (jnp.minimum(s, n_tiles - 1), k))
    prev = lambda s: jnp.maximum(s - 1, 0)
    return pl.pallas_call(
        functools.partial(_gmlp_out_kernel, proj_chunk=proj_chunk),
        grid=(n_tiles + 1,),
        in_specs=[pl.BlockSpec((tm, d), lambda s: (prev(s), 0)),
                  part(0), part(1), part(2),
                  _layer_block(ln_g, layer), _layer_block(ln_b, layer),
                  _layer_block(w_s, layer), _layer_block(b_s_t, layer),
                  _layer_block(w_out, layer),
                  pl.BlockSpec((None, 1, d), lambda s: (prev(s) // per_batch, 0, 0))],
        out_specs=pl.BlockSpec((tm, d), lambda s: (prev(s), 0)),
        out_shape=jax.ShapeDtypeStruct((t, d), F32),
        scratch_shapes=[pltpu.VMEM((GM_GROUPS, CHUNK, CHUNK), BF16),
                        pltpu.VMEM((tm, GM_WIDTH), BF16),
                        pltpu.VMEM((tm, GM_WIDTH), BF16),
                        pltpu.VMEM((tm, GM_WIDTH), BF16)],
        compiler_params=_params(1),
        name="gmlp_out",
    )(x, uvz, uvz, uvz, ln_g, ln_b, w_s, b_s_t, w_out, gate)


def _rows_by_cols_t(a, b):
    return lax.dot_general(a, b, (((1,), (1,)), ((), ())), preferred_element_type=F32)


def _kv_kernel(x_ref, g_ref, sc_ref, sh_ref, wd_ref, gkva_ref, gkr_ref, wk_ref, wvt_ref, gkn_ref,
               cos_ref, shi_ref, slo_ref, sgn_ref, k_ref, vt_ref, *, heads_per_chunk):
    h = _ada_norm(x_ref[...], g_ref[...], sc_ref[...], sh_ref[...])
    ckr = jnp.dot(h, wd_ref[...], preferred_element_type=F32)
    c_kv = (_rms(ckr[:, :KV_LORA_RANK], KV_LORA_RANK) * gkva_ref[...]).astype(BF16)
    k_r = _rms(ckr[:, KV_LORA_RANK:], QK_ROPE_DIM) * gkr_ref[...]
    k_r = _rope(k_r, cos_ref[...], shi_ref[...], slo_ref[...])
    k_x = (k_r + pltpu.roll(k_r, QK_ROPE_DIM, 1) * sgn_ref[...]).astype(BF16)
    for c0 in range(0, N_HEADS, heads_per_chunk):
        heads = slice(c0 * LANES, (c0 + heads_per_chunk) * LANES)
        k_n = jnp.dot(c_kv, wk_ref[:, heads], preferred_element_type=F32)
        v_t = _rows_by_cols_t(wvt_ref[heads, :], c_kv)
        for i in range(heads_per_chunk):
            one = slice(i * LANES, (i + 1) * LANES)
            k_ref[c0 + i, :, :QK_NOPE_DIM] = (_rms(k_n[:, one], QK_NOPE_DIM)
                                              * gkn_ref[...]).astype(BF16)
            k_ref[c0 + i, :, QK_NOPE_DIM:] = k_x
            vt_ref[c0 + i] = v_t[one, :].astype(BF16)


def _kv_proj(x, g, scale, shift, wd, gkva, gkr, wk, wvt, gkn, cos, s_hi, s_lo, sgn, batch, seq,
             tm=MLA_TM, heads_per_chunk=HEADS_PER_CHUNK):
    t, d = x.shape
    per_batch = seq // tm
    vec = pl.BlockSpec((None, 1, d), lambda i: (i // per_batch, 0, 0))
    tab = pl.BlockSpec((tm, LANES), lambda i: (i % per_batch, 0))
    return pl.pallas_call(
        functools.partial(_kv_kernel, heads_per_chunk=heads_per_chunk),
        grid=(t // tm,),
        in_specs=[pl.BlockSpec((tm, d), lambda i: (i, 0)),
                  _resident((1, d)), vec, vec,
                  _resident(wd.shape), _resident(gkva.shape), _resident(gkr.shape),
                  _resident(wk.shape), _resident(wvt.shape), _resident(gkn.shape),
                  tab, tab, tab, _resident(sgn.shape)],
        out_specs=[pl.BlockSpec((None, N_HEADS, tm, HEAD_SLAB),
                                lambda i: (i // per_batch, 0, i % per_batch, 0)),
                   pl.BlockSpec((None, N_HEADS, V_HEAD_DIM, tm),
                                lambda i: (i // per_batch, 0, 0, i % per_batch))],
        out_shape=[jax.ShapeDtypeStruct((batch, N_HEADS, seq, HEAD_SLAB), BF16),
                   jax.ShapeDtypeStruct((batch, N_HEADS, V_HEAD_DIM, seq), BF16)],
        compiler_params=_params(1),
        name="kv_proj",
    )(x, g, scale, shift, wd, gkva, gkr, wk, wvt, gkn, cos, s_hi, s_lo, sgn)


def _col_rms(a):
    return a * lax.rsqrt(jnp.mean(a * a, axis=0, keepdims=True) + EPS)


def _mla_in_kernel(x_ref, g_ref, sc_ref, sh_ref, win_ref, gqa_ref, wuqt_ref, gqn_ref, gqr_ref,
                   tabt_ref, qt_ref, z_ref, *, chunk, heads_per_chunk):
    h = _ada_norm(x_ref[...], g_ref[...], sc_ref[...], sh_ref[...])
    c_q = jnp.dot(h, win_ref[:, :Q_LORA_RANK], preferred_element_type=F32)
    c_q = (_rms(c_q, Q_LORA_RANK) * gqa_ref[...]).astype(BF16)
    for n0 in range(0, ATTN_WIDTH, chunk):
        z = jnp.dot(h, win_ref[:, Q_LORA_RANK + n0:Q_LORA_RANK + n0 + chunk],
                    preferred_element_type=F32)
        z_ref[:, n0:n0 + chunk] = _silu(z).astype(BF16)
    tm = x_ref.shape[0]
    nope_mult = jnp.broadcast_to(gqn_ref[...], (QK_NOPE_DIM, tm))
    rope_mult = tabt_ref[...] * gqr_ref[...]
    for c0 in range(0, N_HEADS, heads_per_chunk):
        q_t = _rows_by_cols_t(wuqt_ref[c0 * HEAD_SLAB:(c0 + heads_per_chunk) * HEAD_SLAB, :],
                              c_q)
        for i in range(heads_per_chunk):
            q_n = q_t[i * HEAD_SLAB:i * HEAD_SLAB + QK_NOPE_DIM, :]
            q_x = q_t[i * HEAD_SLAB + QK_NOPE_DIM:(i + 1) * HEAD_SLAB, :]
            qt_ref[c0 + i, :QK_NOPE_DIM, :] = (_col_rms(q_n) * nope_mult).astype(BF16)
            qt_ref[c0 + i, QK_NOPE_DIM:, :] = (_col_rms(q_x) * rope_mult).astype(BF16)


def _mla_in(x, g, scale, shift, w_in, gqa, wuqt, gqn, gqr, layer, tabt, batch, seq, tm=MLA_TM,
            chunk=MLA_IN_CHUNK, heads_per_chunk=HEADS_PER_CHUNK):
    t, d = x.shape
    per_batch = seq // tm
    vec = pl.BlockSpec((None, 1, d), lambda i: (i // per_batch, 0, 0))
    return pl.pallas_call(
        functools.partial(_mla_in_kernel, chunk=chunk, heads_per_chunk=heads_per_chunk),
        grid=(t // tm,),
        in_specs=[pl.BlockSpec((tm, d), lambda i: (i, 0)),
                  _resident((1, d)), vec, vec,
                  _layer_block(w_in, layer), _layer_block(gqa, layer),
                  _layer_block(wuqt, layer), _layer_block(gqn, layer),
                  _layer_block(gqr, layer),
                  pl.BlockSpec((LANES, tm), lambda i: (0, i % per_batch))],
        out_specs=[pl.BlockSpec((None, N_HEADS, HEAD_SLAB, tm),
                                lambda i: (i // per_batch, 0, 0, i % per_batch)),
                   pl.BlockSpec((tm, ATTN_WIDTH), lambda i: (i, 0))],
        out_shape=[jax.ShapeDtypeStruct((batch, N_HEADS, HEAD_SLAB, seq), BF16),
                   jax.ShapeDtypeStruct((t, ATTN_WIDTH), BF16)],
        compiler_params=_params(1),
        name="mla_in",
    )(x, g, scale, shift, w_in, gqa, wuqt, gqn, gqr, tabt)


def _attn_kernel(qt_ref, k_ref, vt_ref, o_ref, sa_ref, sb_ref, ma_ref, mb_ref, m_ref, l_ref,
                 acc_ref, *, tile):
    qi = pl.program_id(2)
    half = tile // 2

    def causal(s):
        key = lax.broadcasted_iota(jnp.int32, s.shape, 0)
        query = lax.broadcasted_iota(jnp.int32, s.shape, 1)
        return jnp.where(key <= query, s, -jnp.inf)

    def scores(j, s_ref, tile_max_ref):
        k = k_ref[pl.ds(pl.multiple_of(j * tile, tile), tile), :]
        s = jnp.dot(k, qt_ref[...], preferred_element_type=F32)
        s_ref[...] = s
        tile_max_ref[...] = jnp.max(s, axis=0, keepdims=True)

    def update(j, s_ref, tile_max_ref):
        vt = vt_ref[:, pl.ds(pl.multiple_of(j * tile, tile), tile)]
        m = m_ref[...]
        m_new = jnp.maximum(m, tile_max_ref[...])
        alpha = jnp.exp2(m - m_new)
        p = jnp.exp2(s_ref[...] - m_new)
        m_ref[...] = m_new
        l_ref[...] = alpha * l_ref[...] + jnp.sum(p, axis=0, keepdims=True)
        acc_ref[...] = alpha * acc_ref[...] + jnp.dot(vt, p.astype(BF16),
                                                      preferred_element_type=F32)

    def scores_diagonal(j, s_ref, tile_max_ref):
        start = pl.multiple_of(j * tile, tile)
        top = causal(jnp.dot(k_ref[pl.ds(start, half), :], qt_ref[...],
                             preferred_element_type=F32))
        bottom = causal(jnp.dot(k_ref[pl.ds(start + half, half), :], qt_ref[:, half:],
                                preferred_element_type=F32))
        s_ref[:half, :] = top
        s_ref[half:, half:] = bottom
        top_max = jnp.max(top, axis=0, keepdims=True)
        tile_max_ref[:, :half] = top_max[:, :half]
        tile_max_ref[:, half:] = jnp.maximum(top_max[:, half:],
                                             jnp.max(bottom, axis=0, keepdims=True))

    def update_diagonal(j, s_ref, tile_max_ref):
        start = pl.multiple_of(j * tile, tile)
        m = m_ref[...]
        m_new = jnp.maximum(m, tile_max_ref[...])
        alpha = jnp.exp2(m - m_new)
        p_top = jnp.exp2(s_ref[:half, :] - m_new)
        p_bottom = jnp.exp2(s_ref[half:, half:] - m_new[:, half:])
        m_ref[...] = m_new
        top_sum = jnp.sum(p_top, axis=0, keepdims=True)
        pv_top = jnp.dot(vt_ref[:, pl.ds(start, half)], p_top.astype(BF16),
                         preferred_element_type=F32)
        pv_bottom = jnp.dot(vt_ref[:, pl.ds(start + half, half)], p_bottom.astype(BF16),
                            preferred_element_type=F32)
        l_ref[:, :half] = alpha[:, :half] * l_ref[:, :half] + top_sum[:, :half]
        l_ref[:, half:] = (alpha[:, half:] * l_ref[:, half:] + top_sum[:, half:]
                           + jnp.sum(p_bottom, axis=0, keepdims=True))
        acc_ref[:, :half] = alpha[:, :half] * acc_ref[:, :half] + pv_top[:, :half]
        acc_ref[:, half:] = (alpha[:, half:] * acc_ref[:, half:] + pv_top[:, half:]
                             + pv_bottom)

    def finish():
        o_ref[...] = (acc_ref[...] / l_ref[...]).T.astype(o_ref.dtype)

    m_ref[...] = jnp.full(m_ref.shape, -jnp.inf, F32)
    l_ref[...] = jnp.zeros(l_ref.shape, F32)
    acc_ref[...] = jnp.zeros(acc_ref.shape, F32)

    @pl.when(qi > 0)
    def _():
        scores(0, sa_ref, ma_ref)

    @pl.loop(0, jnp.maximum(qi - 1, 0) // 2)
    def _(jj):
        scores(2 * jj + 1, sb_ref, mb_ref)
        update(2 * jj, sa_ref, ma_ref)
        scores(2 * jj + 2, sa_ref, ma_ref)
        update(2 * jj + 1, sb_ref, mb_ref)

    @pl.when(jnp.logical_and(qi > 0, qi % 2 == 0))
    def _():
        scores(qi - 1, sb_ref, mb_ref)
        update(qi - 2, sa_ref, ma_ref)
        scores_diagonal(qi, sa_ref, ma_ref)
        update(qi - 1, sb_ref, mb_ref)
        update_diagonal(qi, sa_ref, ma_ref)
        finish()

    @pl.when(qi % 2 == 1)
    def _():
        scores_diagonal(qi, sb_ref, mb_ref)
        update(qi - 1, sa_ref, ma_ref)
        update_diagonal(qi, sb_ref, mb_ref)
        finish()

    @pl.when(qi == 0)
    def _():
        scores_diagonal(0, sa_ref, ma_ref)
        update_diagonal(0, sa_ref, ma_ref)
        finish()


def _attention(qt, k, vt, tile=ATTN_TILE):
    batch, heads, width, seq = qt.shape
    return pl.pallas_call(
        functools.partial(_attn_kernel, tile=tile),
        grid=(batch, heads, seq // tile),
        in_specs=[pl.BlockSpec((None, None, width, tile), lambda b, h, i: (b, h, 0, i)),
                  pl.BlockSpec((None, None, seq, width), lambda b, h, i: (b, h, 0, 0)),
                  pl.BlockSpec((None, None, V_HEAD_DIM, seq), lambda b, h, i: (b, h, 0, 0))],
        out_specs=pl.BlockSpec((None, tile, V_HEAD_DIM), lambda b, h, i: (b, i, h)),
        out_shape=jax.ShapeDtypeStruct((batch, seq, heads * V_HEAD_DIM), BF16),
        scratch_shapes=[pltpu.VMEM((tile, tile), F32), pltpu.VMEM((tile, tile), F32),
                        pltpu.VMEM((1, tile), F32), pltpu.VMEM((1, tile), F32),
                        pltpu.VMEM((1, tile), F32), pltpu.VMEM((1, tile), F32),
                        pltpu.VMEM((V_HEAD_DIM, tile), F32)],
        compiler_params=_params(3),
        name="attention",
    )(qt, k, vt)


def _mla_out_kernel(x_ref, a_ref, z_ref, w_ref, gate_ref, o_ref):
    y = (a_ref[...].astype(F32) * z_ref[...].astype(F32)).astype(BF16)
    out = jnp.dot(y, w_ref[...], preferred_element_type=F32)
    o_ref[...] = x_ref[...] + gate_ref[...] * out


def _mla_out(x, a, z, w_out, layer, gate, seq, tm=MLA_TM):
    t, d = x.shape
    per_batch = seq // tm
    row = lambda w: pl.BlockSpec((tm, w), lambda i: (i, 0))
    return pl.pallas_call(
        _mla_out_kernel,
        grid=(t // tm,),
        in_specs=[row(d), row(ATTN_WIDTH), row(ATTN_WIDTH),
                  _layer_block(w_out, layer),
                  pl.BlockSpec((None, 1, d), lambda i: (i // per_batch, 0, 0))],
        out_specs=row(d),
        out_shape=jax.ShapeDtypeStruct((t, d), F32),
        compiler_params=_params(1),
        name="mla_out",
    )(x, a, z, w_out, gate)


def _rope_halves(r):
    return r[..., :HALF_ROPE], r[..., HALF_ROPE:]


def _query_slabs(w):
    lead = w.shape[:-1]
    w = w.reshape(*lead, N_HEADS, QK_DIM)
    x1, x2 = _rope_halves(w[..., QK_NOPE_DIM:])
    w = jnp.concatenate([w[..., :QK_NOPE_DIM], x1, x2, x2, x1], axis=-1)
    return w.reshape(*lead, N_HEADS * HEAD_SLAB)


def _rope_tables(seq):
    pos = jnp.arange(seq, dtype=F32)
    inv_freq = ROPE_THETA ** (-jnp.arange(0, QK_ROPE_DIM, 2, dtype=F32) / QK_ROPE_DIM)
    ang = pos[:, None] * inv_freq[None, :]
    cos, sin = jnp.cos(ang), jnp.sin(ang)
    zero = jnp.zeros_like(cos)
    pad = jnp.zeros((seq, LANES - QK_ROPE_DIM), F32)
    key_tables = (jnp.concatenate([cos, cos, pad], axis=1),
                  jnp.concatenate([zero, sin, pad], axis=1),
                  jnp.concatenate([-sin, zero, pad], axis=1))
    query_table = jnp.concatenate([cos, cos, sin, sin], axis=1)
    return key_tables, query_table


def _pad_lanes(g):
    return jnp.pad(g, (0, LANES - g.shape[0])).reshape(1, LANES)


def kernel(x, c, ada_w, ada_b, norm_g, a_w_in, a_ln_g, a_ln_b, a_w_s, a_b_s, a_w_out, kv_ada_w, kv_ada_b, kv_norm_g, kv_w_dkv, kv_g_kva, kv_w_ukv, kv_g_kn, kv_g_kr, b_w_in, b_g_qa, b_w_uq, b_g_qn, b_g_qr, b_w_out):
    batch, seq, d = x.shape
    assert d == D_MODEL and batch <= 8, (batch, d)
    assert seq % max(GMLP_IN_TM, ATTN_TILE, MLA_TM, GMLP_OUT_TM) == 0, seq
    t = batch * seq
    sm_scale = QK_DIM ** -0.5 * math.log2(math.e)

    c_pad = jnp.pad(c, ((0, 8 - batch), (0, 0)))
    mod = _ada_mod(c_pad, ada_w, ada_b)[:, :batch]
    kv_mod = _ada_mod(c_pad, kv_ada_w[None], kv_ada_b[None])[0, :batch]
    vecs = lambda m, n: [v.reshape(batch, 1, d) for v in jnp.split(m, n, axis=-1)]

    (cos, s_hi, s_lo), q_tab = _rope_tables(seq)
    sgn = jnp.concatenate([jnp.zeros((QK_ROPE_DIM,), F32), -jnp.ones((HALF_ROPE,), F32),
                           jnp.ones((HALF_ROPE,), F32)]).reshape(1, LANES)
    xf = x.reshape(t, d)

    a_w_in_bf, a_w_out_bf = (0.5 * a_w_in).astype(BF16), a_w_out.astype(BF16)
    a_ln_g3 = a_ln_g.reshape(N_A_LAYERS, 1, GM_WIDTH)
    a_ln_b3 = a_ln_b.reshape(N_A_LAYERS, 1, GM_WIDTH)
    a_b_s_t = jnp.swapaxes(a_b_s, 1, 2)
    for i in range(N_A_LAYERS):
        shift, scale, gate = vecs(mod[i], 3)
        uvz = _gmlp_in(xf, norm_g[i].reshape(1, d), scale, shift, a_w_in_bf, i, seq)
        xf = _gmlp_out(xf, uvz, a_ln_g3, a_ln_b3, a_w_s, a_b_s_t, a_w_out_bf, i, gate, seq)

    kv_shift, kv_scale = vecs(kv_mod, 2)
    wd = jnp.pad(kv_w_dkv, ((0, 0), (0, LANES - QK_ROPE_DIM))).astype(BF16)
    w_ukv = kv_w_ukv.reshape(KV_LORA_RANK, N_HEADS, QK_NOPE_DIM + V_HEAD_DIM).astype(BF16)
    wk = w_ukv[..., :QK_NOPE_DIM].reshape(KV_LORA_RANK, N_HEADS * QK_NOPE_DIM)
    wvt = w_ukv[..., QK_NOPE_DIM:].reshape(KV_LORA_RANK, N_HEADS * V_HEAD_DIM).T
    k_all, vt_all = _kv_proj(xf, kv_norm_g.reshape(1, d), kv_scale, kv_shift, wd,
                             kv_g_kva.reshape(1, KV_LORA_RANK), _pad_lanes(kv_g_kr),
                             wk, wvt, kv_g_kn.reshape(1, QK_NOPE_DIM),
                             cos, s_hi, s_lo, sgn, batch, seq)

    b_w_in_bf, b_w_out_bf = b_w_in.astype(BF16), b_w_out.astype(BF16)
    b_w_uqt_bf = jnp.swapaxes(_query_slabs(b_w_uq), 1, 2).astype(BF16)
    b_g_qa3 = b_g_qa.reshape(N_B_LAYERS, 1, Q_LORA_RANK)
    b_g_qn3 = (b_g_qn * sm_scale).reshape(N_B_LAYERS, QK_NOPE_DIM, 1)
    g1, g2 = _rope_halves(b_g_qr * sm_scale)
    b_g_qr3 = jnp.concatenate([g1, g2, g2, g1], axis=-1).reshape(N_B_LAYERS, LANES, 1)
    q_tab_t = q_tab.T
    for j in range(N_B_LAYERS):
        shift, scale, gate = vecs(mod[N_A_LAYERS + j], 3)
        q_t, z = _mla_in(xf, norm_g[N_A_LAYERS + j].reshape(1, d), scale, shift, b_w_in_bf,
                         b_g_qa3, b_w_uqt_bf, b_g_qn3, b_g_qr3, j, q_tab_t, batch, seq)
        a = _attention(q_t, k_all, vt_all)
        xf = _mla_out(xf, a.reshape(t, ATTN_WIDTH), z, b_w_out_bf, j, gate, seq)

    return xf.reshape(batch, seq, d)
```
